```python
import jax
import jax.numpy as jnp
from jax import lax
import numpy as np

D_MODEL = 1024
BATCH = 8
SEQ = 4096
DEPTH = 2

GRID_W = 64
CTX_LEN = 256
D_INNER = 2 * D_MODEL
SSD_WIDTH = D_INNER // 2
SSD_HEAD_DIM = 64
SSD_HEADS = SSD_WIDTH // SSD_HEAD_DIM
SSD_GROUPS = 2
SSD_STATE = 128
SSD_TAPS = 5
SSD_CHUNK = 128
CONF_WIDTH = D_INNER // 4
CONF_TAPS = 31
SC_WIDTH = D_INNER // 4
SC_TAPS = 3
D_FF = 2816
N_MOD = 9
MACARON_WEIGHT = 0.5
EPS = 1e-6
GN = SSD_GROUPS * SSD_STATE
XBC_WIDTH = SSD_WIDTH + 2 * GN
SSD_IN = XBC_WIDTH + 2 * SSD_HEADS
IN_WIDTH = SSD_IN + SSD_WIDTH + 2 * CONF_WIDTH + 3 * SC_WIDTH

kernel_name = "hybrid_ssd_conformer_shortconv_dit_block"


def rmsnorm(x, g):
    xf = x.astype(jnp.float32)
    y = xf * lax.rsqrt(jnp.mean(xf * xf, axis=-1, keepdims=True) + EPS)
    return (y * g.astype(jnp.float32)).astype(x.dtype)


def layernorm(x, g, b):
    xf = x.astype(jnp.float32)
    mu = jnp.mean(xf, axis=-1, keepdims=True)
    var = jnp.mean(jnp.square(xf - mu), axis=-1, keepdims=True)
    y = (xf - mu) * lax.rsqrt(var + EPS)
    return (y * g.astype(jnp.float32) + b.astype(jnp.float32)).astype(x.dtype)


def dwconv(x, w, b):
    y = lax.conv_general_dilated(x, w[:, None, :].astype(x.dtype), window_strides=(1,), padding='SAME',
                                 dimension_numbers=('NWC', 'WIO', 'NWC'), feature_group_count=x.shape[-1])
    return y + b.astype(x.dtype)


def conv_along_rows(x, w, b):
    bsz, t, ch = x.shape
    rows = t // GRID_W
    return dwconv(x.reshape(bsz * rows, GRID_W, ch), w, b).reshape(bsz, t, ch)


def conv_along_cols(x, w, b):
    bsz, t, ch = x.shape
    rows = t // GRID_W
    xg = x.reshape(bsz, rows, GRID_W, ch).transpose(0, 2, 1, 3).reshape(bsz * GRID_W, rows, ch)
    y = dwconv(xg, w, b)
    return y.reshape(bsz, GRID_W, rows, ch).transpose(0, 2, 1, 3).reshape(bsz, t, ch)


def segsum(a):
    q = a.shape[-1]
    cs = jnp.cumsum(a, axis=-1)
    diff = cs[..., :, None] - cs[..., None, :]
    mask = jnp.tril(jnp.ones((q, q), dtype=bool))
    return jnp.where(mask, diff, -jnp.inf)


def ssd_scan(x, dt, A, B, C, h0):
    f32 = jnp.float32
    b, l, h, p = x.shape
    g, n = B.shape[-2:]
    r = h // g
    q = SSD_CHUNK
    nc = l // q
    xd = (x.astype(f32) * dt[..., None]).reshape(b, nc, q, g, r, p)
    Bc = B.astype(f32).reshape(b, nc, q, g, n)
    Cc = C.astype(f32).reshape(b, nc, q, g, n)
    a = (dt * A).reshape(b, nc, q, g, r).transpose(0, 3, 4, 1, 2)
    a_cum = jnp.cumsum(a, axis=-1)
    lmat = jnp.exp(segsum(a))
    cb = jnp.einsum('bclgn,bcsgn->bgcls', Cc, Bc)
    y_diag = jnp.einsum('bgcls,bgrcls,bcsgrp->bclgrp', cb, lmat, xd)
    decay_s = jnp.exp(a_cum[..., -1:] - a_cum)
    states = jnp.einsum('bcsgn,bgrcs,bcsgrp->bcgrpn', Bc, decay_s, xd)
    states = jnp.concatenate([h0.reshape(b, 1, g, r, p, n), states], axis=1)
    chunk_a = jnp.pad(a_cum[..., -1], ((0, 0), (0, 0), (0, 0), (1, 0)))
    chunk_decay = jnp.exp(segsum(chunk_a))
    states = jnp.einsum('bgrzc,bcgrpn->bzgrpn', chunk_decay, states)
    prev, final = states[:, :-1], states[:, -1]
    y_off = jnp.einsum('bclgn,bcgrpn,bgrcl->bclgrp', Cc, prev, jnp.exp(a_cum))
    y = (y_diag + y_off).reshape(b, l, h, p)
    return y, final.reshape(b, h, p, n)


def ssd_final_state(x, dt, A, B):
    f32 = jnp.float32
    b, l, h, p = x.shape
    g, n = B.shape[-2:]
    a_cum = jnp.cumsum(dt * A, axis=1)
    w = jnp.exp(a_cum[:, -1:] - a_cum) * dt
    xw = (x.astype(f32) * w[..., None]).reshape(b, l, g, h // g, p)
    st = jnp.einsum('blgn,blgrp->bgrpn', B.astype(f32), xw)
    return st.reshape(b, h, p, n)


def ssd_inputs(u, conv_w, conv_b, dt_bias, a_log):
    b, l, _ = u.shape
    xbc = jax.nn.silu(dwconv(u[..., :XBC_WIDTH], conv_w, conv_b))
    xs = xbc[..., :SSD_WIDTH].reshape(b, l, SSD_HEADS, SSD_HEAD_DIM)
    Bm = xbc[..., SSD_WIDTH:SSD_WIDTH + GN].reshape(b, l, SSD_GROUPS, SSD_STATE)
    Cm = xbc[..., SSD_WIDTH + GN:].reshape(b, l, SSD_GROUPS, SSD_STATE)
    dt_raw = u[..., XBC_WIDTH:].astype(jnp.float32).reshape(b, l, 2, SSD_HEADS)
    dt = jax.nn.softplus(dt_raw + dt_bias.astype(jnp.float32))
    A = -jnp.exp(a_log.astype(jnp.float32))
    return xs, Bm, Cm, dt, A


def flip_seq(t):
    return jnp.flip(t, axis=1)


def ssd_bidir(xs, Bm, Cm, dt, A, h0_f, h0_b):
    y_f, s_f = ssd_scan(xs, dt[:, :, 0], A[0], Bm, Cm, h0_f)
    y_b, s_b = ssd_scan(flip_seq(xs), flip_seq(dt[:, :, 1]), A[1], flip_seq(Bm), flip_seq(Cm), h0_b)
    return y_f + flip_seq(y_b), s_f, s_b


def mixer(u, h0_f, h0_b, conv_conf, conv_sc, lp):
    b, l, _ = u.shape
    o = SSD_IN
    z = u[..., o:o + SSD_WIDTH]
    o += SSD_WIDTH
    conf = u[..., o:o + 2 * CONF_WIDTH]
    o += 2 * CONF_WIDTH
    sc = u[..., o:]
    xs, Bm, Cm, dt, A = ssd_inputs(u[..., :SSD_IN], lp['ssd_conv_w'], lp['ssd_conv_b'], lp['dt_bias'], lp['a_log'])
    y, s_f, s_b = ssd_bidir(xs, Bm, Cm, dt, A, h0_f, h0_b)
    y = (y + lp['d_skip'].astype(jnp.float32)[:, None] * xs.astype(jnp.float32)).astype(u.dtype)
    y_ssd = rmsnorm(y.reshape(b, l, SSD_WIDTH) * jax.nn.silu(z), lp['ssd_norm'])
    v, gt = jnp.split(conf, 2, axis=-1)
    y_conf = jax.nn.silu(layernorm(conv_conf(v * jax.nn.sigmoid(gt), lp['conf_w'], lp['conf_b']), lp['ln_g'], lp['ln_b']))
    bg, cg, hh = jnp.split(sc, 3, axis=-1)
    y_sc = bg * conv_sc(cg * hh, lp['sc_w'], lp['sc_b'])
    return jnp.concatenate([y_ssd, y_conf, y_sc], axis=-1), s_f, s_b


def modulate(h, g_pre, shift, scale):
    return rmsnorm(h, g_pre) * (1 + scale) + shift


def gated_residual(h, y, g_post, gate, weight):
    return h + weight * gate * rmsnorm(y, g_post)


def swiglu_half(h, m, k, g_pre, g_post, wg, wu, wd):
    u = modulate(h, g_pre, m[3 * k], m[3 * k + 1])
    y = (jax.nn.silu(u @ wg) * (u @ wu)) @ wd
    return gated_residual(h, y, g_post, m[3 * k + 2], MACARON_WEIGHT)


def setup_inputs(seed: int = 0) -> dict:
    key = jax.random.key(seed)
    ks = iter(jax.random.split(key, 32))
    nrm = lambda shape, s: jax.random.normal(next(ks), shape, jnp.float32) * s
    L = DEPTH
    dt0 = jnp.exp(jax.random.uniform(next(ks), (L, 2, SSD_HEADS), jnp.float32, np.log(1e-3), np.log(1e-1)))
    return {
        'x': nrm((BATCH, SEQ, D_MODEL), 1.0),
        'c': nrm((BATCH, D_MODEL), 1.0),
        'ctx': nrm((BATCH, CTX_LEN, D_MODEL), 1.0),
        'c_ctx': nrm((D_MODEL,), 1.0),
        'w_mod': nrm((L, D_MODEL, N_MOD * D_MODEL), D_MODEL ** -0.5),
        'b_mod': nrm((L, N_MOD * D_MODEL), 0.02),
        'norm_pre': 1.0 + nrm((L, 3, D_MODEL), 0.05),
        'norm_post': 1.0 + nrm((L, 3, D_MODEL), 0.05),
        'ffn_w_gate': nrm((L, 2, D_MODEL, D_FF), D_MODEL ** -0.5),
        'ffn_w_up': nrm((L, 2, D_MODEL, D_FF), D_MODEL ** -0.5),
        'ffn_w_down': nrm((L, 2, D_FF, D_MODEL), D_FF ** -0.5),
        'w_in': nrm((L, D_MODEL, IN_WIDTH), D_MODEL ** -0.5),
        'ssd_conv_w': nrm((L, SSD_TAPS, XBC_WIDTH), SSD_TAPS ** -0.5),
        'ssd_conv_b': nrm((L, XBC_WIDTH), 0.02),
        'ssd_dt_bias': dt0 + jnp.log(-jnp.expm1(-dt0)),
        'ssd_a_log': jnp.log(jax.random.uniform(next(ks), (L, 2, SSD_HEADS), jnp.float32, 1.0, 16.0)),
        'ssd_d': 1.0 + nrm((L, SSD_HEADS), 0.1),
        'ssd_norm': 1.0 + nrm((L, SSD_WIDTH), 0.05),
        'conf_conv_w': nrm((L, CONF_TAPS, CONF_WIDTH), CONF_TAPS ** -0.5),
        'conf_conv_b': nrm((L, CONF_WIDTH), 0.02),
        'conf_ln_g': 1.0 + nrm((L, CONF_WIDTH), 0.05),
        'conf_ln_b': nrm((L, CONF_WIDTH), 0.02),
        'sc_conv_w': nrm((L, SC_TAPS, SC_WIDTH), SC_TAPS ** -0.5),
        'sc_conv_b': nrm((L, SC_WIDTH), 0.02),
        'w_out': nrm((L, D_INNER, D_MODEL), D_INNER ** -0.5),
    }


def reference(x, c, ctx, c_ctx, w_mod, b_mod, norm_pre, norm_post, ffn_w_gate, ffn_w_up, ffn_w_down,
              w_in, ssd_conv_w, ssd_conv_b, ssd_dt_bias, ssd_a_log, ssd_d, ssd_norm,
              conf_conv_w, conf_conv_b, conf_ln_g, conf_ln_b, sc_conv_w, sc_conv_b, w_out):
    bsz = x.shape[0]
    d = x.shape[-1]
    for i in range(DEPTH):
        last = i == DEPTH - 1
        mx = (jax.nn.silu(c) @ w_mod[i] + b_mod[i]).reshape(bsz, N_MOD, d).transpose(1, 0, 2)[:, :, None, :]
        mc = (jax.nn.silu(c_ctx) @ w_mod[i] + b_mod[i]).reshape(N_MOD, 1, 1, d)
        lp = {'ssd_conv_w': ssd_conv_w[i], 'ssd_conv_b': ssd_conv_b[i], 'dt_bias': ssd_dt_bias[i],
              'a_log': ssd_a_log[i], 'd_skip': ssd_d[i], 'ssd_norm': ssd_norm[i],
              'conf_w': conf_conv_w[i], 'conf_b': conf_conv_b[i], 'ln_g': conf_ln_g[i], 'ln_b': conf_ln_b[i],
              'sc_w': sc_conv_w[i], 'sc_b': sc_conv_b[i]}
        x = swiglu_half(x, mx, 0, norm_pre[i, 0], norm_post[i, 0], ffn_w_gate[i, 0], ffn_w_up[i, 0], ffn_w_down[i, 0])
        ctx = swiglu_half(ctx, mc, 0, norm_pre[i, 0], norm_post[i, 0], ffn_w_gate[i, 0], ffn_w_up[i, 0], ffn_w_down[i, 0])
        uc = modulate(ctx, norm_pre[i, 1], mc[3], mc[4])
        if last:
            xs, Bm, Cm, dt, A = ssd_inputs(uc @ w_in[i][:, :SSD_IN], lp['ssd_conv_w'], lp['ssd_conv_b'], lp['dt_bias'], lp['a_log'])
            s_f = ssd_final_state(xs, dt[:, :, 0], A[0], Bm)
            s_b = ssd_final_state(flip_seq(xs), flip_seq(dt[:, :, 1]), A[1], flip_seq(Bm))
        else:
            zero = jnp.zeros((bsz, SSD_HEADS, SSD_HEAD_DIM, SSD_STATE), jnp.float32)
            yc, s_f, s_b = mixer(uc @ w_in[i], zero, zero, dwconv, dwconv, lp)
            ctx = gated_residual(ctx, yc @ w_out[i], norm_post[i, 1], mc[5], 1.0)
            ctx = swiglu_half(ctx, mc, 2, norm_pre[i, 2], norm_post[i, 2], ffn_w_gate[i, 1], ffn_w_up[i, 1], ffn_w_down[i, 1])
        ux = modulate(x, norm_pre[i, 1], mx[3], mx[4])
        yx, _, _ = mixer(ux @ w_in[i], s_f, s_b, conv_along_rows, conv_along_cols, lp)
        x = gated_residual(x, yx @ w_out[i], norm_post[i, 1], mx[5], 1.0)
        x = swiglu_half(x, mx, 2, norm_pre[i, 2], norm_post[i, 2], ffn_w_gate[i, 1], ffn_w_up[i, 1], ffn_w_down[i, 1])
    return x
```

```python
import functools

import jax
import jax.numpy as jnp
from jax import lax
from jax.experimental import pallas as pl
from jax.experimental.pallas import tpu as pltpu

F32 = jnp.float32
BF16 = jnp.bfloat16

D_MODEL = 1024
GRID_W = 64
D_INNER = 2 * D_MODEL
SSD_WIDTH = D_INNER // 2
SSD_HEAD_DIM = 64
SSD_HEADS = SSD_WIDTH // SSD_HEAD_DIM
SSD_GROUPS = 2
SSD_STATE = 128
SSD_TAPS = 5
SSD_CHUNK = 128
CONF_WIDTH = D_INNER // 4
CONF_TAPS = 31
SC_WIDTH = D_INNER // 4
SC_TAPS = 3
D_FF = 2816
N_MOD = 9
MACARON_WEIGHT = 0.5
EPS = 1e-6
GN = SSD_GROUPS * SSD_STATE
XBC_WIDTH = SSD_WIDTH + 2 * GN
SSD_IN = XBC_WIDTH + 2 * SSD_HEADS
GROUP_WIDTH = SSD_WIDTH // SSD_GROUPS

LANES = 128
SUBLANES = 8
MXU_DIM = 256
VMEM_LIMIT_BYTES = 56 * 1024 * 1024

FFN_CHUNK = MXU_DIM
TOKEN_TILE = 512
HEAD_PAD = LANES


def _rms(x, g):
    return x * lax.rsqrt(jnp.mean(x * x, axis=-1, keepdims=True) + EPS) * g


def _softplus(x):
    return jnp.maximum(x, 0.0) + jnp.log1p(jnp.exp(-jnp.abs(x)))


def _bdot(a, b):
    return jnp.dot(a, b, preferred_element_type=F32)


def _params(*sem):
    return pltpu.CompilerParams(dimension_semantics=sem, vmem_limit_bytes=VMEM_LIMIT_BYTES)


def _const_spec(shape):
    nd = len(shape)
    return pl.BlockSpec(shape, lambda *_: (0,) * nd, pipeline_mode=pl.Buffered(1))


def _mod_kernel(rows_ref, w_ref, b_ref, o_ref):
    s = jax.nn.silu(rows_ref[...])
    o_ref[0] = jnp.dot(s, w_ref[0], precision=lax.Precision.HIGHEST,
                       preferred_element_type=F32) + b_ref[0]


def _mod_call(rows, w_mod, b_mod):
    depth, d, n = w_mod.shape
    r = rows.shape[0]
    tn = n // 8
    return pl.pallas_call(
        _mod_kernel,
        grid=(depth, n // tn),
        in_specs=[pl.BlockSpec((r, d), lambda i, j: (0, 0)),
                  pl.BlockSpec((1, d, tn), lambda i, j: (i, 0, j)),
                  pl.BlockSpec((1, 1, tn), lambda i, j: (i, 0, j))],
        out_specs=pl.BlockSpec((1, r, tn), lambda i, j: (i, 0, j)),
        out_shape=jax.ShapeDtypeStruct((depth, r, n), F32),
        compiler_params=_params("arbitrary", "arbitrary"),
        name="mod",
    )(rows, w_mod, b_mod.reshape(depth, 1, n))


def _ffn_kernel(h_ref, mod_ref, gpre_ref, gpost_ref, wg_ref, wu_ref, wd_ref, o_ref, *, sub):
    x = h_ref[...]
    m = mod_ref[0]
    shift, scale, gate = m[3 * sub:3 * sub + 1], m[3 * sub + 1:3 * sub + 2], m[3 * sub + 2:3 * sub + 3]
    u = (_rms(x, gpre_ref[...]) * (1.0 + scale) + shift).astype(BF16)
    acc = None
    for k in range(D_FF // FFN_CHUNK):
        cols = slice(k * FFN_CHUNK, (k + 1) * FFN_CHUNK)
        a = (jax.nn.silu(_bdot(u, wg_ref[:, cols])) * _bdot(u, wu_ref[:, cols])).astype(BF16)
        part = _bdot(a, wd_ref[cols, :])
        acc = part if acc is None else acc + part
    o_ref[...] = x + MACARON_WEIGHT * gate * _rms(acc, gpost_ref[...])


def _ffn_call(h, mod, mod_row, tile, seq, sub, gpre, gpost, wg, wu, wd):
    n, d = h.shape
    nt = seq // tile
    return pl.pallas_call(
        functools.partial(_ffn_kernel, sub=sub),
        grid=(n // seq, nt),
        in_specs=[pl.BlockSpec((tile, d), lambda b, j: (b * nt + j, 0)),
                  pl.BlockSpec((1, N_MOD, d), lambda b, j: (mod_row(b), 0, 0)),
                  _const_spec((1, d)), _const_spec((1, d)),
                  _const_spec(wg.shape), _const_spec(wu.shape), _const_spec(wd.shape)],
        out_specs=pl.BlockSpec((tile, d), lambda b, j: (b * nt + j, 0)),
        out_shape=jax.ShapeDtypeStruct((n, d), F32),
        compiler_params=_params("parallel", "arbitrary"),
        name="ffn",
    )(h, mod, gpre, gpost, wg, wu, wd)


def _inproj_kernel(h_ref, hb_ref, ha_ref, mod_ref, gpre_ref,
                   wxbc_ref, wdt_ref, wz_ref, wcs_ref,
                   cw_ref, cb_ref, dtb_ref, fw_ref, fb_ref, lng_ref, lnb_ref, sw_ref, sb_ref,
                   xbc_ref, dt_ref, zs_ref, yconf_ref, ysc_ref, *, seg, sc_shift):
    tm = h_ref.shape[0]
    halo = hb_ref.shape[0]
    j = pl.program_id(1)
    has_before = j > 0
    has_after = j < pl.num_programs(1) - 1

    m = mod_ref[0]
    shift, scale = m[3:4], m[4:5]
    gpre = gpre_ref[...]

    def modulated(v):
        return (_rms(v, gpre) * (1.0 + scale) + shift).astype(BF16)

    edge = 2 * SUBLANES

    um = modulated(h_ref[...])
    ub = modulated(hb_ref[...])
    ua = modulated(ha_ref[...])

    wxbc = wxbc_ref[...]
    x_before = jnp.where(has_before, _bdot(ub[halo - edge:], wxbc), 0.0)
    x_after = jnp.where(has_after, _bdot(ua[:edge], wxbc), 0.0)
    xe = jnp.concatenate([x_before, _bdot(um, wxbc), x_after], axis=0)
    acc = cb_ref[...]
    for k in range(SSD_TAPS):
        off = edge - SSD_TAPS // 2 + k
        acc = acc + cw_ref[k:k + 1, :] * xe[off:off + tm]
    xbc_ref[...] = jax.nn.silu(acc)

    dt = _softplus(_bdot(um, wdt_ref[...]) + dtb_ref[...])
    dt_ref[0] = dt[:, :HEAD_PAD]
    dt_ref[1] = dt[:, HEAD_PAD:]

    zs_ref[...] = jax.nn.silu(_bdot(um, wz_ref[...]))

    vg = _bdot(um, wcs_ref[:, :2 * CONF_WIDTH])
    glu = vg[:, :CONF_WIDTH] * jax.nn.sigmoid(vg[:, CONF_WIDTH:])
    runs = tm // seg
    pad = 2 * SUBLANES
    zpad = jnp.zeros((runs, pad, CONF_WIDTH), F32)
    padded = jnp.concatenate([zpad, glu.reshape(runs, seg, CONF_WIDTH), zpad], axis=1)
    conv = None
    for r in range(SUBLANES):
        start = pad - CONF_TAPS // 2 + r
        shifted = padded[:, start:start + seg + 3 * SUBLANES, :]
        for q in range(4):
            k = r + SUBLANES * q
            if k < CONF_TAPS:
                term = fw_ref[k:k + 1, :] * shifted[:, SUBLANES * q:SUBLANES * q + seg, :]
                conv = term if conv is None else conv + term
    conv = conv.reshape(tm, CONF_WIDTH) + fb_ref[...]
    mu = jnp.mean(conv, axis=-1, keepdims=True)
    cen = conv - mu
    var = jnp.mean(cen * cen, axis=-1, keepdims=True)
    yconf_ref[...] = jax.nn.silu(cen * lax.rsqrt(var + EPS) * lng_ref[...] + lnb_ref[...])

    bch = _bdot(um, wcs_ref[:, 2 * CONF_WIDTH:])
    bg = bch[:, :SC_WIDTH]
    prod = bch[:, SC_WIDTH:2 * SC_WIDTH] * bch[:, 2 * SC_WIDTH:]
    hal = _bdot(jnp.concatenate([ub, ua], axis=0), wcs_ref[:, 2 * CONF_WIDTH + SC_WIDTH:])
    hprod = hal[:, :SC_WIDTH] * hal[:, SC_WIDTH:]
    pe = jnp.concatenate([jnp.where(has_before, hprod[:halo], 0.0), prod,
                          jnp.where(has_after, hprod[halo:], 0.0)], axis=0)
    sconv = sb_ref[...]
    for k in range(SC_TAPS):
        off = halo + (k - SC_TAPS // 2) * sc_shift
        sconv = sconv + sw_ref[k:k + 1, :] * pe[off:off + tm]
    ysc_ref[...] = bg * sconv


def _inproj_call(h, mod, mod_row, tile, seq, halo, seg, sc_shift, gpre, lw):
    n, d = h.shape
    nt = seq // tile
    tpb = tile // halo
    last_halo_block = n // halo - 1

    def before_map(b, j):
        return (jnp.maximum((b * nt + j) * tpb - 1, 0), 0)

    def after_map(b, j):
        return (jnp.minimum((b * nt + j + 1) * tpb, last_halo_block), 0)

    def row_map(b, j):
        return (b * nt + j, 0)

    consts = [gpre, lw['w_xbc'], lw['w_dt'], lw['w_z'], lw['w_cs'],
              lw['ssd_conv_w'], lw['ssd_conv_b'], lw['dt_bias'],
              lw['conf_w'], lw['conf_b'], lw['ln_g'], lw['ln_b'], lw['sc_w'], lw['sc_b']]
    return pl.pallas_call(
        functools.partial(_inproj_kernel, seg=seg, sc_shift=sc_shift),
        grid=(n // seq, nt),
        in_specs=[pl.BlockSpec((tile, d), row_map),
                  pl.BlockSpec((halo, d), before_map),
                  pl.BlockSpec((halo, d), after_map),
                  pl.BlockSpec((1, N_MOD, d), lambda b, j: (mod_row(b), 0, 0))]
                 + [_const_spec(a.shape) for a in consts],
        out_specs=[pl.BlockSpec((tile, XBC_WIDTH), row_map),
                   pl.BlockSpec((2, tile, HEAD_PAD), lambda b, j: (0, b * nt + j, 0)),
                   pl.BlockSpec((tile, SSD_WIDTH), row_map),
                   pl.BlockSpec((tile, CONF_WIDTH), row_map),
                   pl.BlockSpec((tile, SC_WIDTH), row_map)],
        out_shape=[jax.ShapeDtypeStruct((n, XBC_WIDTH), F32),
                   jax.ShapeDtypeStruct((2, n, HEAD_PAD), F32),
                   jax.ShapeDtypeStruct((n, SSD_WIDTH), F32),
                   jax.ShapeDtypeStruct((n, CONF_WIDTH), F32),
                   jax.ShapeDtypeStruct((n, SC_WIDTH), F32)],
        compiler_params=_params("parallel", "arbitrary"),
        name="inproj",
    )(h, h, h, mod, *consts)


def _split_dot(v, e):
    hi = v.astype(BF16)
    lo = (v - hi.astype(F32)).astype(BF16)
    return _bdot(hi, e) + _bdot(lo, e)


def _ssd_kernel(xbc_ref, dt_ref, alog_ref, dskip_ref, h0_ref, y_ref, st_ref):
    q = xbc_ref.shape[0]
    fwd = pl.program_id(1) == 0

    @pl.when(pl.program_id(2) == 0)
    def _():
        st_ref[...] = h0_ref[...]

    xbc = xbc_ref[...]
    x = xbc[:, :SSD_WIDTH]
    bm = xbc[:, SSD_WIDTH:SSD_WIDTH + GN].astype(BF16)
    cm = xbc[:, SSD_WIDTH + GN:].astype(BF16)
    dt = dt_ref[0]
    a = dt * (-jnp.exp(alog_ref[0]))

    row = lax.broadcasted_iota(jnp.int32, (q, q), 0)
    col = lax.broadcasted_iota(jnp.int32, (q, q), 1)
    tri = (row >= col).astype(F32)
    cs = jnp.dot(tri, a, precision=lax.Precision.HIGHEST, preferred_element_type=F32)
    tot = cs[q - 1:q]
    w = jnp.where(fwd, cs, tot - cs + a)
    order = jnp.where(fwd, row - col, col - row)
    causal = order >= 0

    sel = (lax.broadcasted_iota(jnp.int32, (2 * SUBLANES, HEAD_PAD), 0)
           == lax.broadcasted_iota(jnp.int32, (2 * SUBLANES, HEAD_PAD), 1)).astype(F32)
    nt_dims = (((1,), (1,)), ((), ()))
    w_t = lax.dot_general(sel, w, nt_dims, precision=lax.Precision.HIGHEST, preferred_element_type=F32)
    dt_t = lax.dot_general(sel, dt, nt_dims, precision=lax.Precision.HIGHEST, preferred_element_type=F32)

    expand = (lax.broadcasted_iota(jnp.int32, (HEAD_PAD, SSD_WIDTH), 0)
              == lax.broadcasted_iota(jnp.int32, (HEAD_PAD, SSD_WIDTH), 1) // SSD_HEAD_DIM).astype(BF16)
    e_in = _split_dot(jnp.exp(w), expand)
    e_out = _split_dot(jnp.exp(tot - w) * dt, expand)
    e_tot = _split_dot(jnp.broadcast_to(jnp.exp(tot), (SUBLANES, HEAD_PAD)), expand)[:1]

    xb = x.astype(BF16)
    xw = (x * e_out).astype(BF16)
    state = st_ref[0, 0]
    state_b = state.astype(BF16)
    lane = lax.broadcasted_iota(jnp.int32, (q, LANES), 1)
    left = lane < SSD_HEAD_DIM

    y_parts = []
    new_state = []
    heads_per_group = SSD_HEADS // SSD_GROUPS
    for g in range(SSD_GROUPS):
        ns = slice(g * SSD_STATE, (g + 1) * SSD_STATE)
        gs = slice(g * GROUP_WIDTH, (g + 1) * GROUP_WIDTH)
        bg, cg = bm[:, ns], cm[:, ns]
        cb = lax.dot_general(cg, bg, nt_dims, preferred_element_type=F32)
        y_off = _bdot(cg, state_b[:, gs]) * e_in[:, gs]
        diag = []
        for pair in range(heads_per_group // 2):
            lhs = []
            for h in (g * heads_per_group + 2 * pair, g * heads_per_group + 2 * pair + 1):
                decay = jnp.exp(jnp.where(causal, w[:, h:h + 1] - w_t[h:h + 1, :], -1e30))
                lhs.append((cb * decay * dt_t[h:h + 1, :]).astype(BF16))
            c0 = g * GROUP_WIDTH + pair * LANES
            xp = xb[:, c0:c0 + LANES]
            zero = jnp.zeros_like(xp)
            rhs = jnp.concatenate([jnp.where(left, xp, zero), jnp.where(left, zero, xp)], axis=0)
            diag.append(_bdot(jnp.concatenate(lhs, axis=1), rhs))
        y_parts.append(jnp.concatenate(diag, axis=1) + y_off)
        upd = lax.dot_general(bg, xw[:, gs], (((0,), (0,)), ((), ())), preferred_element_type=F32)
        new_state.append(e_tot[:, gs] * state[:, gs] + upd)

    y = jnp.concatenate(y_parts, axis=1)
    y_ref[0] = y + jnp.where(fwd, 1.0, 0.0) * dskip_ref[...] * x
    st_ref[0, 0] = jnp.concatenate(new_state, axis=1)


def _ssd_call(xbc, dt, a_log, d_skip, h0, seq):
    n = xbc.shape[0]
    q = SSD_CHUNK
    nc = seq // q

    def chunk(b, d, c):
        return b * nc + c + d * (nc - 1 - 2 * c)

    return pl.pallas_call(
        _ssd_kernel,
        grid=(n // seq, 2, nc),
        in_specs=[pl.BlockSpec((q, XBC_WIDTH), lambda b, d, c: (chunk(b, d, c), 0)),
                  pl.BlockSpec((1, q, HEAD_PAD), lambda b, d, c: (d, chunk(b, d, c), 0)),
                  pl.BlockSpec((1, 1, HEAD_PAD), lambda b, d, c: (d, 0, 0)),
                  pl.BlockSpec((1, SSD_WIDTH), lambda b, d, c: (0, 0)),
                  pl.BlockSpec((1, 1, SSD_STATE, SSD_WIDTH), lambda b, d, c: (b, d, 0, 0))],
        out_specs=[pl.BlockSpec((1, q, SSD_WIDTH), lambda b, d, c: (d, chunk(b, d, c), 0)),
                   pl.BlockSpec((1, 1, SSD_STATE, SSD_WIDTH), lambda b, d, c: (b, d, 0, 0))],
        out_shape=[jax.ShapeDtypeStruct((2, n, SSD_WIDTH), F32),
                   jax.ShapeDtypeStruct(h0.shape, F32)],
        compiler_params=_params("parallel", "arbitrary", "arbitrary"),
        name="ssd",
    )(xbc, dt, a_log, d_skip, h0)


def _post_kernel(h_ref, yf_ref, yb_ref, zs_ref, yconf_ref, ysc_ref, mod_ref, gn_ref, gpost_ref,
                 wo_ssd_ref, wo_conf_ref, wo_sc_ref, o_ref):
    gate = mod_ref[0][5:6]
    y_ssd = _rms((yf_ref[0] + yb_ref[0]) * zs_ref[...], gn_ref[...])
    o = (_bdot(y_ssd.astype(BF16), wo_ssd_ref[...])
         + _bdot(yconf_ref[...].astype(BF16), wo_conf_ref[...])
         + _bdot(ysc_ref[...].astype(BF16), wo_sc_ref[...]))
    o_ref[...] = h_ref[...] + gate * _rms(o, gpost_ref[...])


def _post_call(h, y2, zs, yconf, ysc, mod, mod_row, tile, seq, gn, gpost, lw):
    n, d = h.shape
    nt = seq // tile

    def row_map(b, j):
        return (b * nt + j, 0)

    consts = [gn, gpost, lw['wo_ssd'], lw['wo_conf'], lw['wo_sc']]
    return pl.pallas_call(
        _post_kernel,
        grid=(n // seq, nt),
        in_specs=[pl.BlockSpec((tile, d), row_map),
                  pl.BlockSpec((1, tile, SSD_WIDTH), lambda b, j: (0, b * nt + j, 0)),
                  pl.BlockSpec((1, tile, SSD_WIDTH), lambda b, j: (1, b * nt + j, 0)),
                  pl.BlockSpec((tile, SSD_WIDTH), row_map),
                  pl.BlockSpec((tile, CONF_WIDTH), row_map),
                  pl.BlockSpec((tile, SC_WIDTH), row_map),
                  pl.BlockSpec((1, N_MOD, d), lambda b, j: (mod_row(b), 0, 0))]
                 + [_const_spec(a.shape) for a in consts],
        out_specs=pl.BlockSpec((tile, d), row_map),
        out_shape=jax.ShapeDtypeStruct((n, d), F32),
        compiler_params=_params("parallel", "arbitrary"),
        name="post",
    )(h, y2, y2, zs, yconf, ysc, mod, *consts)


def _pad_lanes(a, width):
    return jnp.pad(a, [(0, 0)] * (a.ndim - 1) + [(0, width - a.shape[-1])])


def _layer_weights(i, w_in, ssd_conv_w, ssd_conv_b, ssd_dt_bias, ssd_a_log, ssd_d, conf_conv_w, conf_conv_b,
                   conf_ln_g, conf_ln_b, sc_conv_w, sc_conv_b, w_out):
    wi = w_in[i]
    dt_cols = wi[:, XBC_WIDTH:SSD_IN]
    w_dt = jnp.concatenate([_pad_lanes(dt_cols[:, :SSD_HEADS], HEAD_PAD),
                            _pad_lanes(dt_cols[:, SSD_HEADS:], HEAD_PAD)], axis=1)
    dtb = ssd_dt_bias[i]
    wo = w_out[i]
    return {
        'w_xbc': wi[:, :XBC_WIDTH].astype(BF16),
        'w_dt': w_dt.astype(BF16),
        'w_z': wi[:, SSD_IN:SSD_IN + SSD_WIDTH].astype(BF16),
        'w_cs': wi[:, SSD_IN + SSD_WIDTH:].astype(BF16),
        'ssd_conv_w': ssd_conv_w[i],
        'ssd_conv_b': ssd_conv_b[i][None],
        'dt_bias': jnp.concatenate([_pad_lanes(dtb[0], HEAD_PAD), _pad_lanes(dtb[1], HEAD_PAD)])[None],
        'a_log': _pad_lanes(ssd_a_log[i], HEAD_PAD)[:, None, :],
        'd_skip': jnp.repeat(ssd_d[i], SSD_HEAD_DIM)[None],
        'conf_w': conf_conv_w[i], 'conf_b': conf_conv_b[i][None],
        'ln_g': conf_ln_g[i][None], 'ln_b': conf_ln_b[i][None],
        'sc_w': sc_conv_w[i], 'sc_b': sc_conv_b[i][None],
        'wo_ssd': wo[:SSD_WIDTH].astype(BF16),
        'wo_conf': wo[SSD_WIDTH:SSD_WIDTH + CONF_WIDTH].astype(BF16),
        'wo_sc': wo[SSD_WIDTH + CONF_WIDTH:].astype(BF16),
    }


def kernel(x, c, ctx, c_ctx, w_mod, b_mod, norm_pre, norm_post, ffn_w_gate, ffn_w_up, ffn_w_down,
           w_in, ssd_conv_w, ssd_conv_b, ssd_dt_bias, ssd_a_log, ssd_d, ssd_norm,
           conf_conv_w, conf_conv_b, conf_ln_g, conf_ln_b, sc_conv_w, sc_conv_b, w_out):
    bsz, seq, d = x.shape
    ctx_len = ctx.shape[1]
    depth = w_mod.shape[0]

    rows = jnp.concatenate([c, c_ctx[None]], axis=0)
    n_rows = -(-rows.shape[0] // SUBLANES) * SUBLANES
    rows = jnp.pad(rows, ((0, n_rows - rows.shape[0]), (0, 0)))
    mod_all = _mod_call(rows, w_mod, b_mod).reshape(depth, n_rows, N_MOD, d)

    def x_row(b):
        return b

    def ctx_row(b):
        return bsz

    h = x.reshape(bsz * seq, d)
    hc = ctx.reshape(bsz * ctx_len, d)
    zero_state = jnp.zeros((bsz, 2, SSD_STATE, SSD_WIDTH), F32)

    for i in range(depth):
        last = i == depth - 1
        mod = mod_all[i]
        lw = _layer_weights(i, w_in, ssd_conv_w, ssd_conv_b, ssd_dt_bias, ssd_a_log, ssd_d, conf_conv_w,
                            conf_conv_b, conf_ln_g, conf_ln_b, sc_conv_w, sc_conv_b, w_out)
        ffn_w = [(ffn_w_gate[i, k].astype(BF16), ffn_w_up[i, k].astype(BF16), ffn_w_down[i, k].astype(BF16))
                 for k in range(2)]
        gpre = [norm_pre[i, k][None] for k in range(3)]
        gpost = [norm_post[i, k][None] for k in range(3)]
        gn = ssd_norm[i][None]

        h = _ffn_call(h, mod, x_row, TOKEN_TILE, seq, 0, gpre[0], gpost[0], *ffn_w[0])
        hc = _ffn_call(hc, mod, ctx_row, ctx_len, ctx_len, 0, gpre[0], gpost[0], *ffn_w[0])

        xbc, dt, zs, yconf, ysc = _inproj_call(hc, mod, ctx_row, ctx_len, ctx_len, 2 * SUBLANES, ctx_len, 1,
                                               gpre[1], lw)
        y2, state = _ssd_call(xbc, dt, lw['a_log'], lw['d_skip'], zero_state, ctx_len)
        if not last:
            hc = _post_call(hc, y2, zs, yconf, ysc, mod, ctx_row, ctx_len, ctx_len, gn, gpost[1], lw)
            hc = _ffn_call(hc, mod, ctx_row, ctx_len, ctx_len, 2, gpre[2], gpost[2], *ffn_w[1])

        xbc, dt, zs, yconf, ysc = _inproj_call(h, mod, x_row, TOKEN_TILE, seq, GRID_W, GRID_W, GRID_W,
                                               gpre[1], lw)
        y2, _ = _ssd_call(xbc, dt, lw['a_log'], lw['d_skip'], state, seq)
        h = _post_call(h, y2, zs, yconf, ysc, mod, x_row, TOKEN_TILE, seq, gn, gpost[1], lw)
        h = _ffn_call(h, mod, x_row, TOKEN_TILE, seq, 2, gpre[2], gpost[2], *ffn_w[1])

    return h.reshape(bsz, seq, d)
```

```python
import functools

import jax
import jax.numpy as jnp
from jax import lax
from jax.experimental import pallas as pl
from jax.experimental.pallas import tpu as pltpu

F32 = jnp.float32
BF16 = jnp.bfloat16

D_MODEL = 1024
GRID_W = 64
D_INNER = 2 * D_MODEL
SSD_WIDTH = D_INNER // 2
SSD_HEAD_DIM = 64
SSD_HEADS = SSD_WIDTH // SSD_HEAD_DIM
SSD_GROUPS = 2
SSD_STATE = 128
SSD_TAPS = 5
SSD_CHUNK = 128
CONF_WIDTH = D_INNER // 4
CONF_TAPS = 31
SC_WIDTH = D_INNER // 4
SC_TAPS = 3
D_FF = 2816
N_MOD = 9
MACARON_WEIGHT = 0.5
EPS = 1e-6
GN = SSD_GROUPS * SSD_STATE
XBC_WIDTH = SSD_WIDTH + 2 * GN
SSD_IN = XBC_WIDTH + 2 * SSD_HEADS
GROUP_WIDTH = SSD_WIDTH // SSD_GROUPS

LANES = 128
SUBLANES = 8
MXU_DIM = 256
VMEM_LIMIT_BYTES = 56 * 1024 * 1024

FFN_CHUNK = MXU_DIM
TOKEN_TILE = 512
HEAD_PAD = LANES
CONV_UNIT = SUBLANES * SUBLANES
CONV_PAD = 2 * SUBLANES
CONV_LIVE_UNITS = 2
CONF_GROUPS_PER_STEP = 2


def _rms(x, g):
    return x * lax.rsqrt(jnp.mean(x * x, axis=-1, keepdims=True) + EPS) * g


def _softplus(x):
    return jnp.maximum(x, 0.0) + jnp.log1p(jnp.exp(-jnp.abs(x)))


def _bdot(a, b):
    return jnp.dot(a, b, preferred_element_type=F32)


def _params(*sem):
    return pltpu.CompilerParams(dimension_semantics=sem, vmem_limit_bytes=VMEM_LIMIT_BYTES)


def _const_spec(shape):
    nd = len(shape)
    return pl.BlockSpec(shape, lambda *_: (0,) * nd, pipeline_mode=pl.Buffered(1))


def _lane_block(cb):
    return slice(cb * LANES, (cb + 1) * LANES)


def _mod_kernel(rows_ref, w_ref, b_ref, o_ref):
    s = jax.nn.silu(rows_ref[...])
    o_ref[0] = jnp.dot(s, w_ref[0], precision=lax.Precision.HIGHEST,
                       preferred_element_type=F32) + b_ref[0]


def _mod_call(rows, w_mod, b_mod):
    depth, d, n = w_mod.shape
    r = rows.shape[0]
    tn = n // 8
    return pl.pallas_call(
        _mod_kernel,
        grid=(depth, n // tn),
        in_specs=[pl.BlockSpec((r, d), lambda i, j: (0, 0)),
                  pl.BlockSpec((1, d, tn), lambda i, j: (i, 0, j)),
                  pl.BlockSpec((1, 1, tn), lambda i, j: (i, 0, j))],
        out_specs=pl.BlockSpec((1, r, tn), lambda i, j: (i, 0, j)),
        out_shape=jax.ShapeDtypeStruct((depth, r, n), F32),
        compiler_params=_params("arbitrary", "arbitrary"),
        name="mod",
    )(rows, w_mod, b_mod.reshape(depth, 1, n))


def _ffn_kernel(h_ref, mod_ref, gpre_ref, gpost_ref, wg_ref, wu_ref, wd_ref, o_ref, *, sub):
    x = h_ref[...]
    m = mod_ref[0]
    shift, scale, gate = m[3 * sub:3 * sub + 1], m[3 * sub + 1:3 * sub + 2], m[3 * sub + 2:3 * sub + 3]
    u = (_rms(x, gpre_ref[...]) * (1.0 + scale) + shift).astype(BF16)
    acc = None
    for k in range(D_FF // FFN_CHUNK):
        cols = slice(k * FFN_CHUNK, (k + 1) * FFN_CHUNK)
        a = (jax.nn.silu(_bdot(u, wg_ref[:, cols])) * _bdot(u, wu_ref[:, cols])).astype(BF16)
        part = _bdot(a, wd_ref[cols, :])
        acc = part if acc is None else acc + part
    o_ref[...] = x + MACARON_WEIGHT * gate * _rms(acc, gpost_ref[...])


def _ffn_call(h, mod, mod_row, tile, seq, sub, gpre, gpost, wg, wu, wd):
    n, d = h.shape
    nt = seq // tile
    return pl.pallas_call(
        functools.partial(_ffn_kernel, sub=sub),
        grid=(n // seq, nt),
        in_specs=[pl.BlockSpec((tile, d), lambda b, j: (b * nt + j, 0)),
                  pl.BlockSpec((1, N_MOD, d), lambda b, j: (mod_row(b), 0, 0)),
                  _const_spec((1, d)), _const_spec((1, d)),
                  _const_spec(wg.shape), _const_spec(wu.shape), _const_spec(wd.shape)],
        out_specs=pl.BlockSpec((tile, d), lambda b, j: (b * nt + j, 0)),
        out_shape=jax.ShapeDtypeStruct((n, d), F32),
        compiler_params=_params("parallel", "arbitrary"),
        name="ffn",
    )(h, mod, gpre, gpost, wg, wu, wd)


def _conv_group(src_ref, dst_ref, w_ref, bias_ref, cb, units, taps, post):
    bias = bias_ref[cb]
    accs = [[None] * SUBLANES for _ in units]
    for k in range(taps):
        wk = jnp.broadcast_to(w_ref[cb, k:k + 1, :], (SUBLANES, LANES))
        for ui, (s0, _) in enumerate(units):
            for r in range(SUBLANES):
                win = src_ref[cb, pl.ds(s0 + (r + k - taps // 2), SUBLANES, stride=SUBLANES), :]
                term = wk * win
                accs[ui][r] = term if k == 0 else accs[ui][r] + term
    for ui, (_, d0) in enumerate(units):
        for r in range(SUBLANES):
            dst_ref[cb, pl.ds(d0 + r, SUBLANES, stride=SUBLANES), :] = post(accs[ui][r] + bias)


XBC_CHUNKS = XBC_WIDTH // MXU_DIM
DT_CHUNK = XBC_CHUNKS
Z_CHUNK0 = DT_CHUNK + 1
BG_CHUNK0 = Z_CHUNK0 + SSD_WIDTH // MXU_DIM
CG_CHUNK0 = BG_CHUNK0 + SC_WIDTH // MXU_DIM
HH_CHUNK0 = CG_CHUNK0 + SC_WIDTH // MXU_DIM
N_CHUNKS = HH_CHUNK0 + SC_WIDTH // MXU_DIM
BLOCKS_PER_CHUNK = MXU_DIM // LANES


def _inproj_kernel(h_ref, hb_ref, ha_ref, mod_ref, gpre_ref, wvg_ref, wrest_ref,
                   cw_ref, cb_ref, dtb_ref, fw_ref, fb_ref, lng_ref, lnb_ref, sw_ref, sb_ref,
                   xbc_ref, dt_ref, zs_ref, yconf_ref, ysc_ref,
                   um_ref, u_ref, xdst_ref, csrc_ref, cdst_ref, *, seg, sc_shift):
    tm = h_ref.shape[0]
    halo = hb_ref.shape[0]
    j = pl.program_id(1)
    has_before = j > 0
    has_after = j < pl.num_programs(1) - 1
    main = slice(CONV_PAD, CONV_PAD + tm)

    m = mod_ref[0]
    shift, scale = m[3:4], m[4:5]
    gpre = gpre_ref[...]

    def modulated(v):
        return (_rms(v, gpre) * (1.0 + scale) + shift).astype(BF16)

    um_ref[...] = modulated(h_ref[...])
    ub = modulated(hb_ref[...])
    ua = modulated(ha_ref[...])

    def project(c):
        part = _bdot(um_ref[...], wrest_ref[c])
        for half in range(BLOCKS_PER_CHUNK):
            u_ref[BLOCKS_PER_CHUNK * c + half, main, :] = part[:, _lane_block(half)]

    edge_rows = jnp.concatenate([ub[halo - CONV_PAD:], ua[:CONV_PAD]], axis=0)
    for c in range(XBC_CHUNKS):
        edges = _bdot(edge_rows, wrest_ref[c])
        before = jnp.where(has_before, edges[:CONV_PAD], 0.0)
        after = jnp.where(has_after, edges[CONV_PAD:], 0.0)
        for half in range(BLOCKS_PER_CHUNK):
            u_ref[BLOCKS_PER_CHUNK * c + half, 0:CONV_PAD, :] = before[:, _lane_block(half)]
            u_ref[BLOCKS_PER_CHUNK * c + half, CONV_PAD + tm:, :] = after[:, _lane_block(half)]

    vg = _bdot(um_ref[...], wvg_ref[...])
    glu = vg[:, :CONF_WIDTH] * jax.nn.sigmoid(vg[:, CONF_WIDTH:])
    run_rows = seg + 2 * CONV_PAD
    zpad = jnp.zeros((CONV_PAD, LANES), F32)
    units = []
    for run in range(tm // seg):
        base = run * run_rows
        for cb in range(CONF_WIDTH // LANES):
            csrc_ref[cb, base:base + CONV_PAD, :] = zpad
            csrc_ref[cb, base + CONV_PAD:base + CONV_PAD + seg, :] = glu[run * seg:(run + 1) * seg, _lane_block(cb)]
            csrc_ref[cb, base + CONV_PAD + seg:base + run_rows, :] = zpad
        units += [(base + CONV_PAD + u * CONV_UNIT, run * seg + u * CONV_UNIT) for u in range(seg // CONV_UNIT)]
    src0, dst0 = units[0]
    src_step, dst_step = units[1][0] - src0, units[1][1] - dst0
    assert all(u == (src0 + i * src_step, dst0 + i * dst_step) for i, u in enumerate(units))

    groups_per_block = len(units) // CONV_LIVE_UNITS
    steps = (CONF_WIDTH // LANES) * groups_per_block // CONF_GROUPS_PER_STEP
    x_groups_per_block = tm // CONV_UNIT // CONV_LIVE_UNITS
    x_groups_per_step = (XBC_WIDTH // LANES) * x_groups_per_block // steps

    def chunk_of(x_group):
        return x_group // x_groups_per_block // BLOCKS_PER_CHUNK

    first = next(f for f in range(N_CHUNKS + 1)
                 if all(chunk_of((i + 1) * x_groups_per_step - 1) < f + i * ((N_CHUNKS - f) // steps)
                        for i in range(steps)))
    chunks_per_step = (N_CHUNKS - first) // steps
    for c in range(first):
        project(c)

    def conv_step(i, carry):
        for t in range(CONF_GROUPS_PER_STEP):
            c_group = i * CONF_GROUPS_PER_STEP + t
            unit0 = (c_group % groups_per_block) * CONV_LIVE_UNITS
            group = [(src0 + (unit0 + ui) * src_step, dst0 + (unit0 + ui) * dst_step)
                     for ui in range(CONV_LIVE_UNITS)]
            _conv_group(csrc_ref, cdst_ref, fw_ref, fb_ref, c_group // groups_per_block, group, CONF_TAPS,
                        lambda v: v)
        for t in range(x_groups_per_step):
            x_group = i * x_groups_per_step + t
            unit0 = (x_group % x_groups_per_block) * CONV_LIVE_UNITS
            group = [(CONV_PAD + (unit0 + ui) * CONV_UNIT, (unit0 + ui) * CONV_UNIT)
                     for ui in range(CONV_LIVE_UNITS)]
            _conv_group(u_ref, xdst_ref, cw_ref, cb_ref, x_group // x_groups_per_block, group, SSD_TAPS,
                        jax.nn.silu)
        for t in range(chunks_per_step):
            project(first + i * chunks_per_step + t)
        return carry

    lax.fori_loop(0, steps, conv_step, 0)
    for c in range(first + steps * chunks_per_step, N_CHUNKS):
        project(c)
    for cb in range(XBC_WIDTH // LANES):
        xbc_ref[:, _lane_block(cb)] = xdst_ref[cb].astype(BF16)

    for d in range(BLOCKS_PER_CHUNK):
        dt_ref[d] = _softplus(u_ref[BLOCKS_PER_CHUNK * DT_CHUNK + d, main, :] + dtb_ref[:, _lane_block(d)])

    for blk in range(SSD_WIDTH // LANES):
        zs_ref[:, _lane_block(blk)] = jax.nn.silu(u_ref[BLOCKS_PER_CHUNK * Z_CHUNK0 + blk, main, :]).astype(BF16)

    halo_rows = jnp.concatenate([ub, ua], axis=0)
    for c in range(SC_WIDTH // MXU_DIM):
        hprod = _bdot(halo_rows, wrest_ref[CG_CHUNK0 + c]) * _bdot(halo_rows, wrest_ref[HH_CHUNK0 + c])
        for half in range(BLOCKS_PER_CHUNK):
            blk = BLOCKS_PER_CHUNK * c + half
            prod = (u_ref[BLOCKS_PER_CHUNK * CG_CHUNK0 + blk, main, :]
                    * u_ref[BLOCKS_PER_CHUNK * HH_CHUNK0 + blk, main, :])
            hp = hprod[:, _lane_block(half)]
            pe = jnp.concatenate([jnp.where(has_before, hp[:halo], 0.0), prod,
                                  jnp.where(has_after, hp[halo:], 0.0)], axis=0)
            sconv = sb_ref[:, _lane_block(blk)]
            for k in range(SC_TAPS):
                off = halo + (k - SC_TAPS // 2) * sc_shift
                sconv = sconv + sw_ref[k:k + 1, _lane_block(blk)] * pe[off:off + tm]
            ysc_ref[:, _lane_block(blk)] = (u_ref[BLOCKS_PER_CHUNK * BG_CHUNK0 + blk, main, :] * sconv).astype(BF16)

    conv = jnp.concatenate([cdst_ref[cb] for cb in range(CONF_WIDTH // LANES)], axis=1)
    mu = jnp.mean(conv, axis=-1, keepdims=True)
    cen = conv - mu
    var = jnp.mean(cen * cen, axis=-1, keepdims=True)
    yconf_ref[...] = jax.nn.silu(cen * lax.rsqrt(var + EPS) * lng_ref[...] + lnb_ref[...]).astype(BF16)


def _inproj_call(h, mod, mod_row, tile, seq, halo, seg, sc_shift, gpre, lw):
    n, d = h.shape
    nt = seq // tile
    tpb = tile // halo
    last_halo_block = n // halo - 1

    def before_map(b, j):
        return (jnp.maximum((b * nt + j) * tpb - 1, 0), 0)

    def after_map(b, j):
        return (jnp.minimum((b * nt + j + 1) * tpb, last_halo_block), 0)

    def row_map(b, j):
        return (b * nt + j, 0)

    consts = [gpre, lw['w_vg'], lw['w_rest'],
              lw['ssd_conv_w'], lw['ssd_conv_b'], lw['dt_bias'],
              lw['conf_w'], lw['conf_b'], lw['ln_g'], lw['ln_b'], lw['sc_w'], lw['sc_b']]
    conf_rows = (tile // seg) * (seg + 2 * CONV_PAD)
    return pl.pallas_call(
        functools.partial(_inproj_kernel, seg=seg, sc_shift=sc_shift),
        grid=(n // seq, nt),
        in_specs=[pl.BlockSpec((tile, d), row_map),
                  pl.BlockSpec((halo, d), before_map),
                  pl.BlockSpec((halo, d), after_map),
                  pl.BlockSpec((1, N_MOD, d), lambda b, j: (mod_row(b), 0, 0))]
                 + [_const_spec(a.shape) for a in consts],
        out_specs=[pl.BlockSpec((tile, XBC_WIDTH), row_map),
                   pl.BlockSpec((2, tile, HEAD_PAD), lambda b, j: (0, b * nt + j, 0)),
                   pl.BlockSpec((tile, SSD_WIDTH), row_map),
                   pl.BlockSpec((tile, CONF_WIDTH), row_map),
                   pl.BlockSpec((tile, SC_WIDTH), row_map)],
        out_shape=[jax.ShapeDtypeStruct((n, XBC_WIDTH), BF16),
                   jax.ShapeDtypeStruct((2, n, HEAD_PAD), F32),
                   jax.ShapeDtypeStruct((n, SSD_WIDTH), BF16),
                   jax.ShapeDtypeStruct((n, CONF_WIDTH), BF16),
                   jax.ShapeDtypeStruct((n, SC_WIDTH), BF16)],
        scratch_shapes=[pltpu.VMEM((tile, d), BF16),
                        pltpu.VMEM((N_CHUNKS * BLOCKS_PER_CHUNK, tile + 2 * CONV_PAD, LANES), F32),
                        pltpu.VMEM((XBC_WIDTH // LANES, tile, LANES), F32),
                        pltpu.VMEM((CONF_WIDTH // LANES, conf_rows, LANES), F32),
                        pltpu.VMEM((CONF_WIDTH // LANES, tile, LANES), F32)],
        compiler_params=_params("parallel", "arbitrary"),
        name="inproj",
    )(h, h, h, mod, *consts)


def _split_dot(v, e):
    hi = v.astype(BF16)
    lo = (v - hi.astype(F32)).astype(BF16)
    return _bdot(hi, e) + _bdot(lo, e)


def _ssd_kernel(xf_ref, xr_ref, dtf_ref, dtr_ref, alog_ref, dskip_ref, h0_ref, yf_ref, yr_ref, st_ref):
    @pl.when(pl.program_id(1) == 0)
    def _():
        st_ref[...] = h0_ref[...]

    q = xf_ref.shape[0]
    x_refs = (xf_ref, xr_ref)
    y_refs = (yf_ref, yr_ref)
    dts = (dtf_ref[0], dtr_ref[0])
    dirs = (0, 1)
    nt_dims = (((1,), (1,)), ((), ()))
    highest = lax.Precision.HIGHEST

    row = lax.broadcasted_iota(jnp.int32, (q, q), 0)
    col = lax.broadcasted_iota(jnp.int32, (q, q), 1)
    tri = (row >= col).astype(F32)
    a = [dts[d] * (-jnp.exp(alog_ref[d])) for d in dirs]
    cs2 = jnp.dot(tri, jnp.concatenate(a, axis=1), precision=highest, preferred_element_type=F32)
    cs = (cs2[:, :HEAD_PAD], cs2[:, HEAD_PAD:])
    tot = [cs[d][q - 1:q] for d in dirs]
    w = (cs[0], tot[1] - cs[1] + a[1])
    causal = (row >= col, row <= col)

    sel = (lax.broadcasted_iota(jnp.int32, (2 * SUBLANES, HEAD_PAD), 0)
           == lax.broadcasted_iota(jnp.int32, (2 * SUBLANES, HEAD_PAD), 1)).astype(F32)
    rows_t = lax.dot_general(sel, jnp.concatenate([w[0], dts[0], w[1], dts[1]], axis=0), nt_dims,
                             precision=highest, preferred_element_type=F32)
    w_t = (rows_t[:, :q], rows_t[:, 2 * q:3 * q])
    dt_t = (rows_t[:, q:2 * q], rows_t[:, 3 * q:])

    expand = (lax.broadcasted_iota(jnp.int32, (HEAD_PAD, SSD_WIDTH), 0)
              == lax.broadcasted_iota(jnp.int32, (HEAD_PAD, SSD_WIDTH), 1) // SSD_HEAD_DIM).astype(BF16)
    scalars = []
    for d in dirs:
        scalars += [jnp.exp(w[d]), jnp.exp(tot[d] - w[d]) * dts[d]]
    scalars += [jnp.broadcast_to(jnp.exp(tot[d]), (SUBLANES, HEAD_PAD)) for d in dirs]
    ex = _split_dot(jnp.concatenate(scalars, axis=0), expand)
    e_in = (ex[:q], ex[2 * q:3 * q])
    e_out = (ex[q:2 * q], ex[3 * q:4 * q])
    e_tot = (ex[4 * q:4 * q + 1], ex[4 * q + SUBLANES:4 * q + SUBLANES + 1])

    left = lax.broadcasted_iota(jnp.int32, (q, LANES), 1) < SSD_HEAD_DIM
    heads_per_group = SSD_HEADS // SSD_GROUPS
    for g in range(SSD_GROUPS):
        gs = slice(g * GROUP_WIDTH, (g + 1) * GROUP_WIDTH)
        b_cols = slice(SSD_WIDTH + g * SSD_STATE, SSD_WIDTH + (g + 1) * SSD_STATE)
        c_cols = slice(SSD_WIDTH + GN + g * SSD_STATE, SSD_WIDTH + GN + (g + 1) * SSD_STATE)
        bg = [x_refs[d][:, b_cols] for d in dirs]
        cg = [x_refs[d][:, c_cols] for d in dirs]
        state = [st_ref[0, d, :, gs] for d in dirs]
        cb = [lax.dot_general(cg[d], bg[d], nt_dims, preferred_element_type=F32) for d in dirs]
        y_off = [_bdot(cg[d], state[d].astype(BF16)) * e_in[d][:, gs] for d in dirs]
        for pair in range(heads_per_group // 2):
            c0 = g * GROUP_WIDTH + pair * LANES
            for d in dirs:
                lhs = []
                for h in (g * heads_per_group + 2 * pair, g * heads_per_group + 2 * pair + 1):
                    decay = jnp.exp(jnp.where(causal[d], w[d][:, h:h + 1] - w_t[d][h:h + 1, :], -1e30))
                    lhs.append((cb[d] * decay * dt_t[d][h:h + 1, :]).astype(BF16))
                xp = x_refs[d][:, c0:c0 + LANES]
                zero = jnp.zeros_like(xp)
                rhs = jnp.concatenate([jnp.where(left, xp, zero), jnp.where(left, zero, xp)], axis=0)
                y = _bdot(jnp.concatenate(lhs, axis=1), rhs) + y_off[d][:, pair * LANES:(pair + 1) * LANES]
                if d == 0:
                    y = y + dskip_ref[:, c0:c0 + LANES] * xp.astype(F32)
                y_refs[d][:, c0:c0 + LANES] = y.astype(BF16)
        for d in dirs:
            xw = (x_refs[d][:, gs].astype(F32) * e_out[d][:, gs]).astype(BF16)
            upd = lax.dot_general(bg[d], xw, (((0,), (0,)), ((), ())), preferred_element_type=F32)
            st_ref[0, d, :, gs] = e_tot[d][:, gs] * state[d] + upd


def _ssd_call(xbc, dt, a_log, d_skip, h0, seq):
    n = xbc.shape[0]
    q = SSD_CHUNK
    nc = seq // q

    def fwd_chunk(b, c):
        return b * nc + c

    def rev_chunk(b, c):
        return b * nc + nc - 1 - c

    return pl.pallas_call(
        _ssd_kernel,
        grid=(n // seq, nc),
        in_specs=[pl.BlockSpec((q, XBC_WIDTH), lambda b, c: (fwd_chunk(b, c), 0)),
                  pl.BlockSpec((q, XBC_WIDTH), lambda b, c: (rev_chunk(b, c), 0)),
                  pl.BlockSpec((1, q, HEAD_PAD), lambda b, c: (0, fwd_chunk(b, c), 0)),
                  pl.BlockSpec((1, q, HEAD_PAD), lambda b, c: (1, rev_chunk(b, c), 0)),
                  pl.BlockSpec((2, 1, HEAD_PAD), lambda b, c: (0, 0, 0)),
                  pl.BlockSpec((1, SSD_WIDTH), lambda b, c: (0, 0)),
                  pl.BlockSpec((1, 2, SSD_STATE, SSD_WIDTH), lambda b, c: (b, 0, 0, 0))],
        out_specs=[pl.BlockSpec((q, SSD_WIDTH), lambda b, c: (fwd_chunk(b, c), 0)),
                   pl.BlockSpec((q, SSD_WIDTH), lambda b, c: (rev_chunk(b, c), 0)),
                   pl.BlockSpec((1, 2, SSD_STATE, SSD_WIDTH), lambda b, c: (b, 0, 0, 0))],
        out_shape=[jax.ShapeDtypeStruct((n, SSD_WIDTH), BF16),
                   jax.ShapeDtypeStruct((n, SSD_WIDTH), BF16),
                   jax.ShapeDtypeStruct(h0.shape, F32)],
        compiler_params=_params("parallel", "arbitrary"),
        name="ssd",
    )(xbc, xbc, dt, dt, a_log, d_skip, h0)


def _post_kernel(h_ref, yf_ref, yr_ref, zs_ref, yconf_ref, ysc_ref, mod_ref, gn_ref, gpost_ref,
                 wo_ssd_ref, wo_conf_ref, wo_sc_ref, o_ref):
    gate = mod_ref[0][5:6]
    y = yf_ref[...].astype(F32) + yr_ref[...].astype(F32)
    y_ssd = _rms(y * zs_ref[...].astype(F32), gn_ref[...])
    o = (_bdot(y_ssd.astype(BF16), wo_ssd_ref[...])
         + _bdot(yconf_ref[...], wo_conf_ref[...])
         + _bdot(ysc_ref[...], wo_sc_ref[...]))
    o_ref[...] = h_ref[...] + gate * _rms(o, gpost_ref[...])


def _post_call(h, yf, yr, zs, yconf, ysc, mod, mod_row, tile, seq, gn, gpost, lw):
    n, d = h.shape
    nt = seq // tile

    def row_map(b, j):
        return (b * nt + j, 0)

    consts = [gn, gpost, lw['wo_ssd'], lw['wo_conf'], lw['wo_sc']]
    return pl.pallas_call(
        _post_kernel,
        grid=(n // seq, nt),
        in_specs=[pl.BlockSpec((tile, d), row_map),
                  pl.BlockSpec((tile, SSD_WIDTH), row_map),
                  pl.BlockSpec((tile, SSD_WIDTH), row_map),
                  pl.BlockSpec((tile, SSD_WIDTH), row_map),
                  pl.BlockSpec((tile, CONF_WIDTH), row_map),
                  pl.BlockSpec((tile, SC_WIDTH), row_map),
                  pl.BlockSpec((1, N_MOD, d), lambda b, j: (mod_row(b), 0, 0))]
                 + [_const_spec(a.shape) for a in consts],
        out_specs=pl.BlockSpec((tile, d), row_map),
        out_shape=jax.ShapeDtypeStruct((n, d), F32),
        compiler_params=_params("parallel", "arbitrary"),
        name="post",
    )(h, yf, yr, zs, yconf, ysc, mod, *consts)


def _pad_lanes(a, width):
    return jnp.pad(a, [(0, 0)] * (a.ndim - 1) + [(0, width - a.shape[-1])])


def _column_chunks(a, width):
    rows, cols = a.shape
    return a.reshape(rows, cols // width, width).transpose(1, 0, 2)


def _layer_weights(i, w_in, ssd_conv_w, ssd_conv_b, ssd_dt_bias, ssd_a_log, ssd_d, conf_conv_w, conf_conv_b,
                   conf_ln_g, conf_ln_b, sc_conv_w, sc_conv_b, w_out):
    wi = w_in[i]
    dt_cols = wi[:, XBC_WIDTH:SSD_IN]
    w_dt = jnp.concatenate([_pad_lanes(dt_cols[:, :SSD_HEADS], HEAD_PAD),
                            _pad_lanes(dt_cols[:, SSD_HEADS:], HEAD_PAD)], axis=1)
    dtb = ssd_dt_bias[i]
    wo = w_out[i]
    conf0 = SSD_IN + SSD_WIDTH
    sc0 = conf0 + 2 * CONF_WIDTH
    w_rest = jnp.concatenate([wi[:, :XBC_WIDTH], w_dt, wi[:, SSD_IN:conf0], wi[:, sc0:]], axis=1)
    return {
        'w_vg': wi[:, conf0:conf0 + 2 * CONF_WIDTH].astype(BF16),
        'w_rest': _column_chunks(w_rest.astype(BF16), MXU_DIM),
        'ssd_conv_w': _column_chunks(ssd_conv_w[i], LANES),
        'ssd_conv_b': _column_chunks(ssd_conv_b[i][None], LANES),
        'dt_bias': jnp.concatenate([_pad_lanes(dtb[0], HEAD_PAD), _pad_lanes(dtb[1], HEAD_PAD)])[None],
        'a_log': _pad_lanes(ssd_a_log[i], HEAD_PAD)[:, None, :],
        'd_skip': jnp.repeat(ssd_d[i], SSD_HEAD_DIM)[None],
        'conf_w': _column_chunks(conf_conv_w[i], LANES),
        'conf_b': _column_chunks(conf_conv_b[i][None], LANES),
        'ln_g': conf_ln_g[i][None], 'ln_b': conf_ln_b[i][None],
        'sc_w': sc_conv_w[i], 'sc_b': sc_conv_b[i][None],
        'wo_ssd': wo[:SSD_WIDTH].astype(BF16),
        'wo_conf': wo[SSD_WIDTH:SSD_WIDTH + CONF_WIDTH].astype(BF16),
        'wo_sc': wo[SSD_WIDTH + CONF_WIDTH:].astype(BF16),
    }


def kernel(x, c, ctx, c_ctx, w_mod, b_mod, norm_pre, norm_post, ffn_w_gate, ffn_w_up, ffn_w_down,
           w_in, ssd_conv_w, ssd_conv_b, ssd_dt_bias, ssd_a_log, ssd_d, ssd_norm,
           conf_conv_w, conf_conv_b, conf_ln_g, conf_ln_b, sc_conv_w, sc_conv_b, w_out):
    bsz, seq, d = x.shape
    ctx_len = ctx.shape[1]
    depth = w_mod.shape[0]

    rows = jnp.concatenate([c, c_ctx[None]], axis=0)
    n_rows = -(-rows.shape[0] // SUBLANES) * SUBLANES
    rows = jnp.pad(rows, ((0, n_rows - rows.shape[0]), (0, 0)))
    mod_all = _mod_call(rows, w_mod, b_mod).reshape(depth, n_rows, N_MOD, d)

    def x_row(b):
        return b

    def ctx_row(b):
        return bsz

    h = x.reshape(bsz * seq, d)
    hc = ctx.reshape(bsz * ctx_len, d)
    zero_state = jnp.zeros((bsz, 2, SSD_STATE, SSD_WIDTH), F32)

    for i in range(depth):
        last = i == depth - 1
        mod = mod_all[i]
        lw = _layer_weights(i, w_in, ssd_conv_w, ssd_conv_b, ssd_dt_bias, ssd_a_log, ssd_d, conf_conv_w,
                            conf_conv_b, conf_ln_g, conf_ln_b, sc_conv_w, sc_conv_b, w_out)
        ffn_w = [(ffn_w_gate[i, k].astype(BF16), ffn_w_up[i, k].astype(BF16), ffn_w_down[i, k].astype(BF16))
                 for k in range(2)]
        gpre = [norm_pre[i, k][None] for k in range(3)]
        gpost = [norm_post[i, k][None] for k in range(3)]
        gn = ssd_norm[i][None]

        h = _ffn_call(h, mod, x_row, TOKEN_TILE, seq, 0, gpre[0], gpost[0], *ffn_w[0])
        hc = _ffn_call(hc, mod, ctx_row, ctx_len, ctx_len, 0, gpre[0], gpost[0], *ffn_w[0])

        xbc, dt, zs, yconf, ysc = _inproj_call(hc, mod, ctx_row, ctx_len, ctx_len, CONV_PAD, ctx_len, 1,
                                               gpre[1], lw)
        yf, yr, state = _ssd_call(xbc, dt, lw['a_log'], lw['d_skip'], zero_state, ctx_len)
        if not last:
            hc = _post_call(hc, yf, yr, zs, yconf, ysc, mod, ctx_row, ctx_len, ctx_len, gn, gpost[1], lw)
            hc = _ffn_call(hc, mod, ctx_row, ctx_len, ctx_len, 2, gpre[2], gpost[2], *ffn_w[1])

        xbc, dt, zs, yconf, ysc = _inproj_call(h, mod, x_row, TOKEN_TILE, seq, GRID_W, GRID_W, GRID_W,
                                               gpre[1], lw)
        yf, yr, _ = _ssd_call(xbc, dt, lw['a_log'], lw['d_skip'], state, seq)
        h = _post_call(h, yf, yr, zs, yconf, ysc, mod, x_row, TOKEN_TILE, seq, gn, gpost[1], lw)
        h = _ffn_call(h, mod, x_row, TOKEN_TILE, seq, 2, gpre[2], gpost[2], *ffn_w[1])

    return h.reshape(bsz, seq, d)
```

```python
import functools

import jax
import jax.numpy as jnp
from jax import lax
from jax.experimental import pallas as pl
from jax.experimental.pallas import tpu as pltpu

F32 = jnp.float32
BF16 = jnp.bfloat16

D_MODEL = 1024
GRID_W = 64
D_INNER = 2 * D_MODEL
SSD_WIDTH = D_INNER // 2
SSD_HEAD_DIM = 64
SSD_HEADS = SSD_WIDTH // SSD_HEAD_DIM
SSD_GROUPS = 2
SSD_STATE = 128
SSD_TAPS = 5
SSD_CHUNK = 128
CONF_WIDTH = D_INNER // 4
CONF_TAPS = 31
SC_WIDTH = D_INNER // 4
SC_TAPS = 3
D_FF = 2816
N_MOD = 9
MACARON_WEIGHT = 0.5
EPS = 1e-6
GN = SSD_GROUPS * SSD_STATE
XBC_WIDTH = SSD_WIDTH + 2 * GN
SSD_IN = XBC_WIDTH + 2 * SSD_HEADS
GROUP_WIDTH = SSD_WIDTH // SSD_GROUPS

LANES = 128
SUBLANES = 8
MXU_DIM = 256
VMEM_LIMIT_BYTES = 56 * 1024 * 1024

FFN_CHUNK = MXU_DIM
TOKEN_TILE = 512
HEAD_PAD = LANES
CONV_STRIDE = 4
CONV_UNIT = SUBLANES * CONV_STRIDE
CONV_PAD = 2 * SUBLANES
CONV_LIVE_UNITS = 4
CONF_GROUPS_PER_STEP = 2


def _rms(x, g):
    return x * lax.rsqrt(jnp.mean(x * x, axis=-1, keepdims=True) + EPS) * g


def _softplus(x):
    return jnp.maximum(x, 0.0) + jnp.log1p(jnp.exp(-jnp.abs(x)))


def _bdot(a, b):
    return jnp.dot(a, b, preferred_element_type=F32)


def _params(*sem):
    return pltpu.CompilerParams(dimension_semantics=sem, vmem_limit_bytes=VMEM_LIMIT_BYTES)


def _const_spec(shape):
    nd = len(shape)
    return pl.BlockSpec(shape, lambda *_: (0,) * nd, pipeline_mode=pl.Buffered(1))


def _lane_block(cb):
    return slice(cb * LANES, (cb + 1) * LANES)


def _mod_kernel(rows_ref, w_ref, b_ref, o_ref):
    s = jax.nn.silu(rows_ref[...])
    o_ref[0] = jnp.dot(s, w_ref[0], precision=lax.Precision.HIGHEST,
                       preferred_element_type=F32) + b_ref[0]


def _mod_call(rows, w_mod, b_mod):
    depth, d, n = w_mod.shape
    r = rows.shape[0]
    tn = n // 8
    return pl.pallas_call(
        _mod_kernel,
        grid=(depth, n // tn),
        in_specs=[pl.BlockSpec((r, d), lambda i, j: (0, 0)),
                  pl.BlockSpec((1, d, tn), lambda i, j: (i, 0, j)),
                  pl.BlockSpec((1, 1, tn), lambda i, j: (i, 0, j))],
        out_specs=pl.BlockSpec((1, r, tn), lambda i, j: (i, 0, j)),
        out_shape=jax.ShapeDtypeStruct((depth, r, n), F32),
        compiler_params=_params("arbitrary", "arbitrary"),
        name="mod",
    )(rows, w_mod, b_mod.reshape(depth, 1, n))


def _ffn_kernel(h_ref, mod_ref, gpre_ref, gpost_ref, wg_ref, wu_ref, wd_ref, o_ref, *, sub):
    x = h_ref[...]
    m = mod_ref[0]
    shift, scale, gate = m[3 * sub:3 * sub + 1], m[3 * sub + 1:3 * sub + 2], m[3 * sub + 2:3 * sub + 3]
    u = (_rms(x, gpre_ref[...]) * (1.0 + scale) + shift).astype(BF16)
    acc = None
    for k in range(D_FF // FFN_CHUNK):
        cols = slice(k * FFN_CHUNK, (k + 1) * FFN_CHUNK)
        a = (jax.nn.silu(_bdot(u, wg_ref[:, cols])) * _bdot(u, wu_ref[:, cols])).astype(BF16)
        part = _bdot(a, wd_ref[cols, :])
        acc = part if acc is None else acc + part
    o_ref[...] = x + MACARON_WEIGHT * gate * _rms(acc, gpost_ref[...])


def _ffn_call(h, mod, mod_row, tile, seq, sub, gpre, gpost, wg, wu, wd):
    n, d = h.shape
    nt = seq // tile
    return pl.pallas_call(
        functools.partial(_ffn_kernel, sub=sub),
        grid=(n // seq, nt),
        in_specs=[pl.BlockSpec((tile, d), lambda b, j: (b * nt + j, 0)),
                  pl.BlockSpec((1, N_MOD, d), lambda b, j: (mod_row(b), 0, 0)),
                  _const_spec((1, d)), _const_spec((1, d)),
                  _const_spec(wg.shape), _const_spec(wu.shape), _const_spec(wd.shape)],
        out_specs=pl.BlockSpec((tile, d), lambda b, j: (b * nt + j, 0)),
        out_shape=jax.ShapeDtypeStruct((n, d), F32),
        compiler_params=_params("parallel", "arbitrary"),
        name="ffn",
    )(h, mod, gpre, gpost, wg, wu, wd)


def _conv_group(src_ref, dst_ref, w_ref, bias_ref, cb, units, taps, post):
    bias = bias_ref[cb]
    accs = [[None] * CONV_STRIDE for _ in units]
    for k in range(taps):
        wk = jnp.broadcast_to(w_ref[cb, k:k + 1, :], (SUBLANES, LANES))
        for ui, (s0, _) in enumerate(units):
            for r in range(CONV_STRIDE):
                win = src_ref[cb, pl.ds(s0 + (r + k - taps // 2), SUBLANES, stride=CONV_STRIDE), :]
                term = wk * win
                accs[ui][r] = term if k == 0 else accs[ui][r] + term
    for ui, (_, d0) in enumerate(units):
        for r in range(CONV_STRIDE):
            dst_ref[cb, pl.ds(d0 + r, SUBLANES, stride=CONV_STRIDE), :] = post(accs[ui][r] + bias)


XBC_CHUNKS = XBC_WIDTH // MXU_DIM
DT_CHUNK = XBC_CHUNKS
Z_CHUNK0 = DT_CHUNK + 1
BG_CHUNK0 = Z_CHUNK0 + SSD_WIDTH // MXU_DIM
CG_CHUNK0 = BG_CHUNK0 + SC_WIDTH // MXU_DIM
HH_CHUNK0 = CG_CHUNK0 + SC_WIDTH // MXU_DIM
N_CHUNKS = HH_CHUNK0 + SC_WIDTH // MXU_DIM
BLOCKS_PER_CHUNK = MXU_DIM // LANES


def _inproj_kernel(h_ref, hb_ref, ha_ref, mod_ref, gpre_ref, wvg_ref, wrest_ref,
                   cw_ref, cb_ref, dtb_ref, fw_ref, fb_ref, lng_ref, lnb_ref, sw_ref, sb_ref,
                   xbc_ref, dt_ref, zs_ref, yconf_ref, ysc_ref,
                   um_ref, u_ref, xdst_ref, csrc_ref, cdst_ref, *, seg, sc_shift):
    tm = h_ref.shape[0]
    halo = hb_ref.shape[0]
    j = pl.program_id(1)
    has_before = j > 0
    has_after = j < pl.num_programs(1) - 1
    main = slice(CONV_PAD, CONV_PAD + tm)

    m = mod_ref[0]
    shift, scale = m[3:4], m[4:5]
    gpre = gpre_ref[...]

    def modulated(v):
        return (_rms(v, gpre) * (1.0 + scale) + shift).astype(BF16)

    um_ref[...] = modulated(h_ref[...])
    ub = modulated(hb_ref[...])
    ua = modulated(ha_ref[...])

    def project(c):
        part = _bdot(um_ref[...], wrest_ref[c])
        for half in range(BLOCKS_PER_CHUNK):
            u_ref[BLOCKS_PER_CHUNK * c + half, main, :] = part[:, _lane_block(half)]

    edge_rows = jnp.concatenate([ub[halo - CONV_PAD:], ua[:CONV_PAD]], axis=0)
    for c in range(XBC_CHUNKS):
        edges = _bdot(edge_rows, wrest_ref[c])
        before = jnp.where(has_before, edges[:CONV_PAD], 0.0)
        after = jnp.where(has_after, edges[CONV_PAD:], 0.0)
        for half in range(BLOCKS_PER_CHUNK):
            u_ref[BLOCKS_PER_CHUNK * c + half, 0:CONV_PAD, :] = before[:, _lane_block(half)]
            u_ref[BLOCKS_PER_CHUNK * c + half, CONV_PAD + tm:, :] = after[:, _lane_block(half)]

    vg = _bdot(um_ref[...], wvg_ref[...])
    glu = vg[:, :CONF_WIDTH] * jax.nn.sigmoid(vg[:, CONF_WIDTH:])
    run_rows = seg + 2 * CONV_PAD
    zpad = jnp.zeros((CONV_PAD, LANES), F32)
    units = []
    for run in range(tm // seg):
        base = run * run_rows
        for cb in range(CONF_WIDTH // LANES):
            csrc_ref[cb, base:base + CONV_PAD, :] = zpad
            csrc_ref[cb, base + CONV_PAD:base + CONV_PAD + seg, :] = glu[run * seg:(run + 1) * seg, _lane_block(cb)]
            csrc_ref[cb, base + CONV_PAD + seg:base + run_rows, :] = zpad
        units += [(base + CONV_PAD + u * CONV_UNIT, run * seg + u * CONV_UNIT) for u in range(seg // CONV_UNIT)]
    src0, dst0 = units[0]
    rel = [(s - src0, t - dst0) for s, t in units[:CONV_LIVE_UNITS]]
    src_step, dst_step = [a - b for a, b in zip(units[CONV_LIVE_UNITS % len(units)], units[0])]
    assert all(units[g0 + ui] == (src0 + g0 // CONV_LIVE_UNITS * src_step + rs,
                                  dst0 + g0 // CONV_LIVE_UNITS * dst_step + rd)
               for g0 in range(0, len(units), CONV_LIVE_UNITS) for ui, (rs, rd) in enumerate(rel))

    groups_per_block = len(units) // CONV_LIVE_UNITS
    steps = (CONF_WIDTH // LANES) * groups_per_block // CONF_GROUPS_PER_STEP
    x_groups_per_block = tm // CONV_UNIT // CONV_LIVE_UNITS
    x_groups_per_step = (XBC_WIDTH // LANES) * x_groups_per_block // steps

    def chunk_of(x_group):
        return x_group // x_groups_per_block // BLOCKS_PER_CHUNK

    first = next(f for f in range(N_CHUNKS + 1)
                 if all(chunk_of((i + 1) * x_groups_per_step - 1) < f + i * ((N_CHUNKS - f) // steps)
                        for i in range(steps)))
    chunks_per_step = (N_CHUNKS - first) // steps
    for c in range(first):
        project(c)

    def conv_step(i, carry):
        for t in range(CONF_GROUPS_PER_STEP):
            c_group = i * CONF_GROUPS_PER_STEP + t
            g = c_group % groups_per_block
            group = [(src0 + g * src_step + rs, dst0 + g * dst_step + rd) for rs, rd in rel]
            _conv_group(csrc_ref, cdst_ref, fw_ref, fb_ref, c_group // groups_per_block, group, CONF_TAPS,
                        lambda v: v)
        for t in range(x_groups_per_step):
            x_group = i * x_groups_per_step + t
            unit0 = (x_group % x_groups_per_block) * CONV_LIVE_UNITS
            group = [(CONV_PAD + (unit0 + ui) * CONV_UNIT, (unit0 + ui) * CONV_UNIT)
                     for ui in range(CONV_LIVE_UNITS)]
            _conv_group(u_ref, xdst_ref, cw_ref, cb_ref, x_group // x_groups_per_block, group, SSD_TAPS,
                        jax.nn.silu)
        for t in range(chunks_per_step):
            project(first + i * chunks_per_step + t)
        return carry

    lax.fori_loop(0, steps, conv_step, 0)
    for c in range(first + steps * chunks_per_step, N_CHUNKS):
        project(c)
    for cb in range(XBC_WIDTH // LANES):
        xbc_ref[:, _lane_block(cb)] = xdst_ref[cb].astype(BF16)

    for d in range(BLOCKS_PER_CHUNK):
        dt_ref[d] = _softplus(u_ref[BLOCKS_PER_CHUNK * DT_CHUNK + d, main, :] + dtb_ref[:, _lane_block(d)])

    for blk in range(SSD_WIDTH // LANES):
        zs_ref[:, _lane_block(blk)] = jax.nn.silu(u_ref[BLOCKS_PER_CHUNK * Z_CHUNK0 + blk, main, :]).astype(BF16)

    halo_rows = jnp.concatenate([ub, ua], axis=0)
    for c in range(SC_WIDTH // MXU_DIM):
        hprod = _bdot(halo_rows, wrest_ref[CG_CHUNK0 + c]) * _bdot(halo_rows, wrest_ref[HH_CHUNK0 + c])
        for half in range(BLOCKS_PER_CHUNK):
            blk = BLOCKS_PER_CHUNK * c + half
            prod = (u_ref[BLOCKS_PER_CHUNK * CG_CHUNK0 + blk, main, :]
                    * u_ref[BLOCKS_PER_CHUNK * HH_CHUNK0 + blk, main, :])
            hp = hprod[:, _lane_block(half)]
            pe = jnp.concatenate([jnp.where(has_before, hp[:halo], 0.0), prod,
                                  jnp.where(has_after, hp[halo:], 0.0)], axis=0)
            sconv = sb_ref[:, _lane_block(blk)]
            for k in range(SC_TAPS):
                off = halo + (k - SC_TAPS // 2) * sc_shift
                sconv = sconv + sw_ref[k:k + 1, _lane_block(blk)] * pe[off:off + tm]
            ysc_ref[:, _lane_block(blk)] = (u_ref[BLOCKS_PER_CHUNK * BG_CHUNK0 + blk, main, :] * sconv).astype(BF16)

    conv = jnp.concatenate([cdst_ref[cb] for cb in range(CONF_WIDTH // LANES)], axis=1)
    mu = jnp.mean(conv, axis=-1, keepdims=True)
    cen = conv - mu
    var = jnp.mean(cen * cen, axis=-1, keepdims=True)
    yconf_ref[...] = jax.nn.silu(cen * lax.rsqrt(var + EPS) * lng_ref[...] + lnb_ref[...]).astype(BF16)


def _inproj_call(h, mod, mod_row, tile, seq, halo, seg, sc_shift, gpre, lw):
    n, d = h.shape
    nt = seq // tile
    tpb = tile // halo
    last_halo_block = n // halo - 1

    def before_map(b, j):
        return (jnp.maximum((b * nt + j) * tpb - 1, 0), 0)

    def after_map(b, j):
        return (jnp.minimum((b * nt + j + 1) * tpb, last_halo_block), 0)

    def row_map(b, j):
        return (b * nt + j, 0)

    consts = [gpre, lw['w_vg'], lw['w_rest'],
              lw['ssd_conv_w'], lw['ssd_conv_b'], lw['dt_bias'],
              lw['conf_w'], lw['conf_b'], lw['ln_g'], lw['ln_b'], lw['sc_w'], lw['sc_b']]
    conf_rows = (tile // seg) * (seg + 2 * CONV_PAD)
    return pl.pallas_call(
        functools.partial(_inproj_kernel, seg=seg, sc_shift=sc_shift),
        grid=(n // seq, nt),
        in_specs=[pl.BlockSpec((tile, d), row_map),
                  pl.BlockSpec((halo, d), before_map),
                  pl.BlockSpec((halo, d), after_map),
                  pl.BlockSpec((1, N_MOD, d), lambda b, j: (mod_row(b), 0, 0))]
                 + [_const_spec(a.shape) for a in consts],
        out_specs=[pl.BlockSpec((tile, XBC_WIDTH), row_map),
                   pl.BlockSpec((2, tile, HEAD_PAD), lambda b, j: (0, b * nt + j, 0)),
                   pl.BlockSpec((tile, SSD_WIDTH), row_map),
                   pl.BlockSpec((tile, CONF_WIDTH), row_map),
                   pl.BlockSpec((tile, SC_WIDTH), row_map)],
        out_shape=[jax.ShapeDtypeStruct((n, XBC_WIDTH), BF16),
                   jax.ShapeDtypeStruct((2, n, HEAD_PAD), F32),
                   jax.ShapeDtypeStruct((n, SSD_WIDTH), BF16),
                   jax.ShapeDtypeStruct((n, CONF_WIDTH), BF16),
                   jax.ShapeDtypeStruct((n, SC_WIDTH), BF16)],
        scratch_shapes=[pltpu.VMEM((tile, d), BF16),
                        pltpu.VMEM((N_CHUNKS * BLOCKS_PER_CHUNK, tile + 2 * CONV_PAD, LANES), F32),
                        pltpu.VMEM((XBC_WIDTH // LANES, tile, LANES), F32),
                        pltpu.VMEM((CONF_WIDTH // LANES, conf_rows, LANES), F32),
                        pltpu.VMEM((CONF_WIDTH // LANES, tile, LANES), F32)],
        compiler_params=_params("parallel", "arbitrary"),
        name="inproj",
    )(h, h, h, mod, *consts)


def _split_dot(v, e):
    hi = v.astype(BF16)
    lo = (v - hi.astype(F32)).astype(BF16)
    return _bdot(hi, e) + _bdot(lo, e)


def _ssd_kernel(xf_ref, xr_ref, dtf_ref, dtr_ref, alog_ref, dskip_ref, h0_ref, yf_ref, yr_ref, st_ref):
    @pl.when(pl.program_id(1) == 0)
    def _():
        st_ref[...] = h0_ref[...]

    q = xf_ref.shape[0]
    x_refs = (xf_ref, xr_ref)
    y_refs = (yf_ref, yr_ref)
    dts = (dtf_ref[0], dtr_ref[0])
    dirs = (0, 1)
    nt_dims = (((1,), (1,)), ((), ()))
    highest = lax.Precision.HIGHEST

    row = lax.broadcasted_iota(jnp.int32, (q, q), 0)
    col = lax.broadcasted_iota(jnp.int32, (q, q), 1)
    tri = (row >= col).astype(F32)
    a = [dts[d] * (-jnp.exp(alog_ref[d])) for d in dirs]
    cs2 = jnp.dot(tri, jnp.concatenate(a, axis=1), precision=highest, preferred_element_type=F32)
    cs = (cs2[:, :HEAD_PAD], cs2[:, HEAD_PAD:])
    tot = [cs[d][q - 1:q] for d in dirs]
    w = (cs[0], tot[1] - cs[1] + a[1])
    causal = (row >= col, row <= col)

    sel = (lax.broadcasted_iota(jnp.int32, (2 * SUBLANES, HEAD_PAD), 0)
           == lax.broadcasted_iota(jnp.int32, (2 * SUBLANES, HEAD_PAD), 1)).astype(F32)
    rows_t = lax.dot_general(sel, jnp.concatenate([w[0], dts[0], w[1], dts[1]], axis=0), nt_dims,
                             precision=highest, preferred_element_type=F32)
    w_t = (rows_t[:, :q], rows_t[:, 2 * q:3 * q])
    dt_t = (rows_t[:, q:2 * q], rows_t[:, 3 * q:])

    expand = (lax.broadcasted_iota(jnp.int32, (HEAD_PAD, SSD_WIDTH), 0)
              == lax.broadcasted_iota(jnp.int32, (HEAD_PAD, SSD_WIDTH), 1) // SSD_HEAD_DIM).astype(BF16)
    scalars = []
    for d in dirs:
        scalars += [jnp.exp(w[d]), jnp.exp(tot[d] - w[d]) * dts[d]]
    scalars += [jnp.broadcast_to(jnp.exp(tot[d]), (SUBLANES, HEAD_PAD)) for d in dirs]
    ex = _split_dot(jnp.concatenate(scalars, axis=0), expand)
    e_in = (ex[:q], ex[2 * q:3 * q])
    e_out = (ex[q:2 * q], ex[3 * q:4 * q])
    e_tot = (ex[4 * q:4 * q + 1], ex[4 * q + SUBLANES:4 * q + SUBLANES + 1])

    left = lax.broadcasted_iota(jnp.int32, (q, LANES), 1) < SSD_HEAD_DIM
    heads_per_group = SSD_HEADS // SSD_GROUPS
    for g in range(SSD_GROUPS):
        gs = slice(g * GROUP_WIDTH, (g + 1) * GROUP_WIDTH)
        b_cols = slice(SSD_WIDTH + g * SSD_STATE, SSD_WIDTH + (g + 1) * SSD_STATE)
        c_cols = slice(SSD_WIDTH + GN + g * SSD_STATE, SSD_WIDTH + GN + (g + 1) * SSD_STATE)
        bg = [x_refs[d][:, b_cols] for d in dirs]
        cg = [x_refs[d][:, c_cols] for d in dirs]
        state = [st_ref[0, d, :, gs] for d in dirs]
        cb = [lax.dot_general(cg[d], bg[d], nt_dims, preferred_element_type=F32) for d in dirs]
        y_off = [_bdot(cg[d], state[d].astype(BF16)) * e_in[d][:, gs] for d in dirs]
        for pair in range(heads_per_group // 2):
            c0 = g * GROUP_WIDTH + pair * LANES
            for d in dirs:
                lhs = []
                for h in (g * heads_per_group + 2 * pair, g * heads_per_group + 2 * pair + 1):
                    decay = jnp.exp(jnp.where(causal[d], w[d][:, h:h + 1] - w_t[d][h:h + 1, :], -1e30))
                    lhs.append((cb[d] * decay * dt_t[d][h:h + 1, :]).astype(BF16))
                xp = x_refs[d][:, c0:c0 + LANES]
                zero = jnp.zeros_like(xp)
                rhs = jnp.concatenate([jnp.where(left, xp, zero), jnp.where(left, zero, xp)], axis=0)
                y = _bdot(jnp.concatenate(lhs, axis=1), rhs) + y_off[d][:, pair * LANES:(pair + 1) * LANES]
                if d == 0:
                    y = y + dskip_ref[:, c0:c0 + LANES] * xp.astype(F32)
                y_refs[d][:, c0:c0 + LANES] = y.astype(BF16)
        for d in dirs:
            xw = (x_refs[d][:, gs].astype(F32) * e_out[d][:, gs]).astype(BF16)
            upd = lax.dot_general(bg[d], xw, (((0,), (0,)), ((), ())), preferred_element_type=F32)
            st_ref[0, d, :, gs] = e_tot[d][:, gs] * state[d] + upd


def _ssd_call(xbc, dt, a_log, d_skip, h0, seq):
    n = xbc.shape[0]
    q = SSD_CHUNK
    nc = seq // q

    def fwd_chunk(b, c):
        return b * nc + c

    def rev_chunk(b, c):
        return b * nc + nc - 1 - c

    return pl.pallas_call(
        _ssd_kernel,
        grid=(n // seq, nc),
        in_specs=[pl.BlockSpec((q, XBC_WIDTH), lambda b, c: (fwd_chunk(b, c), 0)),
                  pl.BlockSpec((q, XBC_WIDTH), lambda b, c: (rev_chunk(b, c), 0)),
                  pl.BlockSpec((1, q, HEAD_PAD), lambda b, c: (0, fwd_chunk(b, c), 0)),
                  pl.BlockSpec((1, q, HEAD_PAD), lambda b, c: (1, rev_chunk(b, c), 0)),
                  pl.BlockSpec((2, 1, HEAD_PAD), lambda b, c: (0, 0, 0)),
                  pl.BlockSpec((1, SSD_WIDTH), lambda b, c: (0, 0)),
                  pl.BlockSpec((1, 2, SSD_STATE, SSD_WIDTH), lambda b, c: (b, 0, 0, 0))],
        out_specs=[pl.BlockSpec((q, SSD_WIDTH), lambda b, c: (fwd_chunk(b, c), 0)),
                   pl.BlockSpec((q, SSD_WIDTH), lambda b, c: (rev_chunk(b, c), 0)),
                   pl.BlockSpec((1, 2, SSD_STATE, SSD_WIDTH), lambda b, c: (b, 0, 0, 0))],
        out_shape=[jax.ShapeDtypeStruct((n, SSD_WIDTH), BF16),
                   jax.ShapeDtypeStruct((n, SSD_WIDTH), BF16),
                   jax.ShapeDtypeStruct(h0.shape, F32)],
        compiler_params=_params("parallel", "arbitrary"),
        name="ssd",
    )(xbc, xbc, dt, dt, a_log, d_skip, h0)


def _post_kernel(h_ref, yf_ref, yr_ref, zs_ref, yconf_ref, ysc_ref, mod_ref, gn_ref, gpost_ref,
                 wo_ssd_ref, wo_conf_ref, wo_sc_ref, o_ref):
    gate = mod_ref[0][5:6]
    y = yf_ref[...].astype(F32) + yr_ref[...].astype(F32)
    y_ssd = _rms(y * zs_ref[...].astype(F32), gn_ref[...])
    o = (_bdot(y_ssd.astype(BF16), wo_ssd_ref[...])
         + _bdot(yconf_ref[...], wo_conf_ref[...])
         + _bdot(ysc_ref[...], wo_sc_ref[...]))
    o_ref[...] = h_ref[...] + gate * _rms(o, gpost_ref[...])


def _post_call(h, yf, yr, zs, yconf, ysc, mod, mod_row, tile, seq, gn, gpost, lw):
    n, d = h.shape
    nt = seq // tile

    def row_map(b, j):
        return (b * nt + j, 0)

    consts = [gn, gpost, lw['wo_ssd'], lw['wo_conf'], lw['wo_sc']]
    return pl.pallas_call(
        _post_kernel,
        grid=(n // seq, nt),
        in_specs=[pl.BlockSpec((tile, d), row_map),
                  pl.BlockSpec((tile, SSD_WIDTH), row_map),
                  pl.BlockSpec((tile, SSD_WIDTH), row_map),
                  pl.BlockSpec((tile, SSD_WIDTH), row_map),
                  pl.BlockSpec((tile, CONF_WIDTH), row_map),
                  pl.BlockSpec((tile, SC_WIDTH), row_map),
                  pl.BlockSpec((1, N_MOD, d), lambda b, j: (mod_row(b), 0, 0))]
                 + [_const_spec(a.shape) for a in consts],
        out_specs=pl.BlockSpec((tile, d), row_map),
        out_shape=jax.ShapeDtypeStruct((n, d), F32),
        compiler_params=_params("parallel", "arbitrary"),
        name="post",
    )(h, yf, yr, zs, yconf, ysc, mod, *consts)


def _pad_lanes(a, width):
    return jnp.pad(a, [(0, 0)] * (a.ndim - 1) + [(0, width - a.shape[-1])])


def _column_chunks(a, width):
    rows, cols = a.shape
    return a.reshape(rows, cols // width, width).transpose(1, 0, 2)


def _layer_weights(i, w_in, ssd_conv_w, ssd_conv_b, ssd_dt_bias, ssd_a_log, ssd_d, conf_conv_w, conf_conv_b,
                   conf_ln_g, conf_ln_b, sc_conv_w, sc_conv_b, w_out):
    wi = w_in[i]
    dt_cols = wi[:, XBC_WIDTH:SSD_IN]
    w_dt = jnp.concatenate([_pad_lanes(dt_cols[:, :SSD_HEADS], HEAD_PAD),
                            _pad_lanes(dt_cols[:, SSD_HEADS:], HEAD_PAD)], axis=1)
    dtb = ssd_dt_bias[i]
    wo = w_out[i]
    conf0 = SSD_IN + SSD_WIDTH
    sc0 = conf0 + 2 * CONF_WIDTH
    w_rest = jnp.concatenate([wi[:, :XBC_WIDTH], w_dt, wi[:, SSD_IN:conf0], wi[:, sc0:]], axis=1)
    return {
        'w_vg': wi[:, conf0:conf0 + 2 * CONF_WIDTH].astype(BF16),
        'w_rest': _column_chunks(w_rest.astype(BF16), MXU_DIM),
        'ssd_conv_w': _column_chunks(ssd_conv_w[i], LANES),
        'ssd_conv_b': _column_chunks(ssd_conv_b[i][None], LANES),
        'dt_bias': jnp.concatenate([_pad_lanes(dtb[0], HEAD_PAD), _pad_lanes(dtb[1], HEAD_PAD)])[None],
        'a_log': _pad_lanes(ssd_a_log[i], HEAD_PAD)[:, None, :],
        'd_skip': jnp.repeat(ssd_d[i], SSD_HEAD_DIM)[None],
        'conf_w': _column_chunks(conf_conv_w[i], LANES),
        'conf_b': _column_chunks(conf_conv_b[i][None], LANES),
        'ln_g': conf_ln_g[i][None], 'ln_b': conf_ln_b[i][None],
        'sc_w': sc_conv_w[i], 'sc_b': sc_conv_b[i][None],
        'wo_ssd': wo[:SSD_WIDTH].astype(BF16),
        'wo_conf': wo[SSD_WIDTH:SSD_WIDTH + CONF_WIDTH].astype(BF16),
        'wo_sc': wo[SSD_WIDTH + CONF_WIDTH:].astype(BF16),
    }


def kernel(x, c, ctx, c_ctx, w_mod, b_mod, norm_pre, norm_post, ffn_w_gate, ffn_w_up, ffn_w_down,
           w_in, ssd_conv_w, ssd_conv_b, ssd_dt_bias, ssd_a_log, ssd_d, ssd_norm,
           conf_conv_w, conf_conv_b, conf_ln_g, conf_ln_b, sc_conv_w, sc_conv_b, w_out):
    bsz, seq, d = x.shape
    ctx_len = ctx.shape[1]
    depth = w_mod.shape[0]

    rows = jnp.concatenate([c, c_ctx[None]], axis=0)
    n_rows = -(-rows.shape[0] // SUBLANES) * SUBLANES
    rows = jnp.pad(rows, ((0, n_rows - rows.shape[0]), (0, 0)))
    mod_all = _mod_call(rows, w_mod, b_mod).reshape(depth, n_rows, N_MOD, d)

    def x_row(b):
        return b

    def ctx_row(b):
        return bsz

    h = x.reshape(bsz * seq, d)
    hc = ctx.reshape(bsz * ctx_len, d)
    zero_state = jnp.zeros((bsz, 2, SSD_STATE, SSD_WIDTH), F32)

    for i in range(depth):
        last = i == depth - 1
        mod = mod_all[i]
        lw = _layer_weights(i, w_in, ssd_conv_w, ssd_conv_b, ssd_dt_bias, ssd_a_log, ssd_d, conf_conv_w,
                            conf_conv_b, conf_ln_g, conf_ln_b, sc_conv_w, sc_conv_b, w_out)
        ffn_w = [(ffn_w_gate[i, k].astype(BF16), ffn_w_up[i, k].astype(BF16), ffn_w_down[i, k].astype(BF16))
                 for k in range(2)]
        gpre = [norm_pre[i, k][None] for k in range(3)]
        gpost = [norm_post[i, k][None] for k in range(3)]
        gn = ssd_norm[i][None]

        h = _ffn_call(h, mod, x_row, TOKEN_TILE, seq, 0, gpre[0], gpost[0], *ffn_w[0])
        hc = _ffn_call(hc, mod, ctx_row, ctx_len, ctx_len, 0, gpre[0], gpost[0], *ffn_w[0])

        xbc, dt, zs, yconf, ysc = _inproj_call(hc, mod, ctx_row, ctx_len, ctx_len, CONV_PAD, ctx_len, 1,
                                               gpre[1], lw)
        yf, yr, state = _ssd_call(xbc, dt, lw['a_log'], lw['d_skip'], zero_state, ctx_len)
        if not last:
            hc = _post_call(hc, yf, yr, zs, yconf, ysc, mod, ctx_row, ctx_len, ctx_len, gn, gpost[1], lw)
            hc = _ffn_call(hc, mod, ctx_row, ctx_len, ctx_len, 2, gpre[2], gpost[2], *ffn_w[1])

        xbc, dt, zs, yconf, ysc = _inproj_call(h, mod, x_row, TOKEN_TILE, seq, GRID_W, GRID_W, GRID_W,
                                               gpre[1], lw)
        yf, yr, _ = _ssd_call(xbc, dt, lw['a_log'], lw['d_skip'], state, seq)
        h = _post_call(h, yf, yr, zs, yconf, ysc, mod, x_row, TOKEN_TILE, seq, gn, gpost[1], lw)
        h = _ffn_call(h, mod, x_row, TOKEN_TILE, seq, 2, gpre[2], gpost[2], *ffn_w[1])

    return h.reshape(bsz, seq, d)
```

```python
import functools

import jax
import jax.numpy as jnp
from jax import lax
from jax.experimental import pallas as pl
from jax.experimental.pallas import tpu as pltpu

F32 = jnp.float32
BF16 = jnp.bfloat16

D_MODEL = 1024
GRID_W = 64
D_INNER = 2 * D_MODEL
SSD_WIDTH = D_INNER // 2
SSD_HEAD_DIM = 64
SSD_HEADS = SSD_WIDTH // SSD_HEAD_DIM
SSD_GROUPS = 2
SSD_STATE = 128
SSD_TAPS = 5
SSD_CHUNK = 128
CONF_WIDTH = D_INNER // 4
CONF_TAPS = 31
SC_WIDTH = D_INNER // 4
SC_TAPS = 3
D_FF = 2816
N_MOD = 9
MACARON_WEIGHT = 0.5
EPS = 1e-6
GN = SSD_GROUPS * SSD_STATE
XBC_WIDTH = SSD_WIDTH + 2 * GN
SSD_IN = XBC_WIDTH + 2 * SSD_HEADS
GROUP_WIDTH = SSD_WIDTH // SSD_GROUPS

LANES = 128
SUBLANES = 8
MXU_DIM = 256
VMEM_LIMIT_BYTES = 56 * 1024 * 1024

FFN_CHUNK = MXU_DIM
TOKEN_TILE = 512
HEAD_PAD = LANES
CONV_STRIDE = 4
CONV_UNIT = SUBLANES * CONV_STRIDE
CONV_PAD = 2 * SUBLANES
CONV_LIVE_UNITS = 4
CONF_GROUPS_PER_STEP = 2


def _rms(x, g):
    return x * lax.rsqrt(jnp.mean(x * x, axis=-1, keepdims=True) + EPS) * g


def _softplus(x):
    return jnp.maximum(x, 0.0) + jnp.log1p(jnp.exp(-jnp.abs(x)))


def _bdot(a, b):
    return jnp.dot(a, b, preferred_element_type=F32)


def _params(*sem):
    return pltpu.CompilerParams(dimension_semantics=sem, vmem_limit_bytes=VMEM_LIMIT_BYTES)


def _const_spec(shape):
    nd = len(shape)
    return pl.BlockSpec(shape, lambda *_: (0,) * nd, pipeline_mode=pl.Buffered(1))


def _lane_block(cb):
    return slice(cb * LANES, (cb + 1) * LANES)


def _cast_kernel(x_ref, o_ref):
    o_ref[...] = x_ref[...].astype(BF16)


def _to_bf16(a, row_blocks):
    rows, cols = a.shape[-2:]
    a3 = a.reshape((-1, rows, cols))
    tr = rows // row_blocks
    out = pl.pallas_call(
        _cast_kernel,
        grid=(a3.shape[0], row_blocks),
        in_specs=[pl.BlockSpec((1, tr, cols), lambda i, j: (i, j, 0))],
        out_specs=pl.BlockSpec((1, tr, cols), lambda i, j: (i, j, 0)),
        out_shape=jax.ShapeDtypeStruct(a3.shape, BF16),
        compiler_params=_params("parallel", "arbitrary"),
        name="cast",
    )(a3)
    return out.reshape(a.shape)


def _mod_kernel(rows_ref, w_ref, b_ref, o_ref):
    s = jax.nn.silu(rows_ref[...])
    o_ref[0] = jnp.dot(s, w_ref[0], precision=lax.Precision.HIGHEST,
                       preferred_element_type=F32) + b_ref[0]


def _mod_call(rows, w_mod, b_mod):
    depth, d, n = w_mod.shape
    r = rows.shape[0]
    tn = n // 8
    return pl.pallas_call(
        _mod_kernel,
        grid=(depth, n // tn),
        in_specs=[pl.BlockSpec((r, d), lambda i, j: (0, 0)),
                  pl.BlockSpec((1, d, tn), lambda i, j: (i, 0, j)),
                  pl.BlockSpec((1, 1, tn), lambda i, j: (i, 0, j))],
        out_specs=pl.BlockSpec((1, r, tn), lambda i, j: (i, 0, j)),
        out_shape=jax.ShapeDtypeStruct((depth, r, n), F32),
        compiler_params=_params("arbitrary", "arbitrary"),
        name="mod",
    )(rows, w_mod, b_mod.reshape(depth, 1, n))


def _ffn_kernel(h_ref, mod_ref, gpre_ref, gpost_ref, wg_ref, wu_ref, wd_ref, o_ref, *, sub):
    x = h_ref[...]
    m = mod_ref[0]
    shift, scale, gate = m[3 * sub:3 * sub + 1], m[3 * sub + 1:3 * sub + 2], m[3 * sub + 2:3 * sub + 3]
    u = (_rms(x, gpre_ref[...]) * (1.0 + scale) + shift).astype(BF16)
    acc = None
    for k in range(D_FF // FFN_CHUNK):
        cols = slice(k * FFN_CHUNK, (k + 1) * FFN_CHUNK)
        a = (jax.nn.silu(_bdot(u, wg_ref[:, cols])) * _bdot(u, wu_ref[:, cols])).astype(BF16)
        part = _bdot(a, wd_ref[cols, :])
        acc = part if acc is None else acc + part
    o_ref[...] = x + MACARON_WEIGHT * gate * _rms(acc, gpost_ref[...])


def _ffn_call(h, mod, mod_row, tile, seq, sub, gpre, gpost, wg, wu, wd):
    n, d = h.shape
    nt = seq // tile
    return pl.pallas_call(
        functools.partial(_ffn_kernel, sub=sub),
        grid=(n // seq, nt),
        in_specs=[pl.BlockSpec((tile, d), lambda b, j: (b * nt + j, 0)),
                  pl.BlockSpec((1, N_MOD, d), lambda b, j: (mod_row(b), 0, 0)),
                  _const_spec((1, d)), _const_spec((1, d)),
                  _const_spec(wg.shape), _const_spec(wu.shape), _const_spec(wd.shape)],
        out_specs=pl.BlockSpec((tile, d), lambda b, j: (b * nt + j, 0)),
        out_shape=jax.ShapeDtypeStruct((n, d), F32),
        compiler_params=_params("parallel", "arbitrary"),
        name="ffn",
    )(h, mod, gpre, gpost, wg, wu, wd)


def _conv_group(src_ref, dst_ref, w_ref, bias_ref, cb, units, taps, post):
    bias = bias_ref[cb]
    accs = [[None] * CONV_STRIDE for _ in units]
    for k in range(taps):
        wk = jnp.broadcast_to(w_ref[cb, k:k + 1, :], (SUBLANES, LANES))
        for ui, (s0, _) in enumerate(units):
            for r in range(CONV_STRIDE):
                win = src_ref[cb, pl.ds(s0 + (r + k - taps // 2), SUBLANES, stride=CONV_STRIDE), :]
                term = wk * win
                accs[ui][r] = term if k == 0 else accs[ui][r] + term
    for ui, (_, d0) in enumerate(units):
        for r in range(CONV_STRIDE):
            dst_ref[cb, pl.ds(d0 + r, SUBLANES, stride=CONV_STRIDE), :] = post(accs[ui][r] + bias)


XBC_CHUNKS = XBC_WIDTH // MXU_DIM
DT_CHUNK = XBC_CHUNKS
Z_CHUNK0 = DT_CHUNK + 1
BG_CHUNK0 = Z_CHUNK0 + SSD_WIDTH // MXU_DIM
CG_CHUNK0 = BG_CHUNK0 + SC_WIDTH // MXU_DIM
HH_CHUNK0 = CG_CHUNK0 + SC_WIDTH // MXU_DIM
N_CHUNKS = HH_CHUNK0 + SC_WIDTH // MXU_DIM
BLOCKS_PER_CHUNK = MXU_DIM // LANES


def _inproj_kernel(h_ref, hb_ref, ha_ref, mod_ref, gpre_ref, wvg_ref, wrest_ref,
                   cw_ref, cb_ref, dtb_ref, fw_ref, fb_ref, lng_ref, lnb_ref, sw_ref, sb_ref,
                   xbc_ref, dt_ref, zs_ref, yconf_ref, ysc_ref,
                   um_ref, u_ref, xdst_ref, csrc_ref, cdst_ref, *, seg, sc_shift):
    tm = h_ref.shape[0]
    halo = hb_ref.shape[0]
    j = pl.program_id(1)
    has_before = j > 0
    has_after = j < pl.num_programs(1) - 1
    main = slice(CONV_PAD, CONV_PAD + tm)

    m = mod_ref[0]
    shift, scale = m[3:4], m[4:5]
    gpre = gpre_ref[...]

    def modulated(v):
        return (_rms(v, gpre) * (1.0 + scale) + shift).astype(BF16)

    um_ref[...] = modulated(h_ref[...])
    ub = modulated(hb_ref[...])
    ua = modulated(ha_ref[...])

    def project(c):
        part = _bdot(um_ref[...], wrest_ref[c])
        for half in range(BLOCKS_PER_CHUNK):
            u_ref[BLOCKS_PER_CHUNK * c + half, main, :] = part[:, _lane_block(half)]

    edge_rows = jnp.concatenate([ub[halo - CONV_PAD:], ua[:CONV_PAD]], axis=0)
    for c in range(XBC_CHUNKS):
        edges = _bdot(edge_rows, wrest_ref[c])
        before = jnp.where(has_before, edges[:CONV_PAD], 0.0)
        after = jnp.where(has_after, edges[CONV_PAD:], 0.0)
        for half in range(BLOCKS_PER_CHUNK):
            u_ref[BLOCKS_PER_CHUNK * c + half, 0:CONV_PAD, :] = before[:, _lane_block(half)]
            u_ref[BLOCKS_PER_CHUNK * c + half, CONV_PAD + tm:, :] = after[:, _lane_block(half)]

    vg = _bdot(um_ref[...], wvg_ref[...])
    glu = vg[:, :CONF_WIDTH] * jax.nn.sigmoid(vg[:, CONF_WIDTH:])
    run_rows = seg + 2 * CONV_PAD
    zpad = jnp.zeros((CONV_PAD, LANES), F32)
    units = []
    for run in range(tm // seg):
        base = run * run_rows
        for cb in range(CONF_WIDTH // LANES):
            csrc_ref[cb, base:base + CONV_PAD, :] = zpad
            csrc_ref[cb, base + CONV_PAD:base + CONV_PAD + seg, :] = glu[run * seg:(run + 1) * seg, _lane_block(cb)]
            csrc_ref[cb, base + CONV_PAD + seg:base + run_rows, :] = zpad
        units += [(base + CONV_PAD + u * CONV_UNIT, run * seg + u * CONV_UNIT) for u in range(seg // CONV_UNIT)]
    src0, dst0 = units[0]
    rel = [(s - src0, t - dst0) for s, t in units[:CONV_LIVE_UNITS]]
    src_step, dst_step = [a - b for a, b in zip(units[CONV_LIVE_UNITS % len(units)], units[0])]
    assert all(units[g0 + ui] == (src0 + g0 // CONV_LIVE_UNITS * src_step + rs,
                                  dst0 + g0 // CONV_LIVE_UNITS * dst_step + rd)
               for g0 in range(0, len(units), CONV_LIVE_UNITS) for ui, (rs, rd) in enumerate(rel))

    groups_per_block = len(units) // CONV_LIVE_UNITS
    steps = (CONF_WIDTH // LANES) * groups_per_block // CONF_GROUPS_PER_STEP
    x_groups_per_block = tm // CONV_UNIT // CONV_LIVE_UNITS
    x_groups_per_step = (XBC_WIDTH // LANES) * x_groups_per_block // steps

    def chunk_of(x_group):
        return x_group // x_groups_per_block // BLOCKS_PER_CHUNK

    first = next(f for f in range(N_CHUNKS + 1)
                 if all(chunk_of((i + 1) * x_groups_per_step - 1) < f + i * ((N_CHUNKS - f) // steps)
                        for i in range(steps)))
    chunks_per_step = (N_CHUNKS - first) // steps
    for c in range(first):
        project(c)

    def conv_step(i, carry):
        for t in range(CONF_GROUPS_PER_STEP):
            c_group = i * CONF_GROUPS_PER_STEP + t
            g = c_group % groups_per_block
            group = [(src0 + g * src_step + rs, dst0 + g * dst_step + rd) for rs, rd in rel]
            _conv_group(csrc_ref, cdst_ref, fw_ref, fb_ref, c_group // groups_per_block, group, CONF_TAPS,
                        lambda v: v)
        for t in range(x_groups_per_step):
            x_group = i * x_groups_per_step + t
            unit0 = (x_group % x_groups_per_block) * CONV_LIVE_UNITS
            group = [(CONV_PAD + (unit0 + ui) * CONV_UNIT, (unit0 + ui) * CONV_UNIT)
                     for ui in range(CONV_LIVE_UNITS)]
            _conv_group(u_ref, xdst_ref, cw_ref, cb_ref, x_group // x_groups_per_block, group, SSD_TAPS,
                        jax.nn.silu)
        for t in range(chunks_per_step):
            project(first + i * chunks_per_step + t)
        return carry

    lax.fori_loop(0, steps, conv_step, 0)
    for c in range(first + steps * chunks_per_step, N_CHUNKS):
        project(c)
    for cb in range(XBC_WIDTH // LANES):
        xbc_ref[:, _lane_block(cb)] = xdst_ref[cb].astype(BF16)

    for d in range(BLOCKS_PER_CHUNK):
        dt_ref[d] = _softplus(u_ref[BLOCKS_PER_CHUNK * DT_CHUNK + d, main, :] + dtb_ref[:, _lane_block(d)])

    for blk in range(SSD_WIDTH // LANES):
        zs_ref[:, _lane_block(blk)] = jax.nn.silu(u_ref[BLOCKS_PER_CHUNK * Z_CHUNK0 + blk, main, :]).astype(BF16)

    halo_rows = jnp.concatenate([ub, ua], axis=0)
    for c in range(SC_WIDTH // MXU_DIM):
        hprod = _bdot(halo_rows, wrest_ref[CG_CHUNK0 + c]) * _bdot(halo_rows, wrest_ref[HH_CHUNK0 + c])
        for half in range(BLOCKS_PER_CHUNK):
            blk = BLOCKS_PER_CHUNK * c + half
            prod = (u_ref[BLOCKS_PER_CHUNK * CG_CHUNK0 + blk, main, :]
                    * u_ref[BLOCKS_PER_CHUNK * HH_CHUNK0 + blk, main, :])
            hp = hprod[:, _lane_block(half)]
            pe = jnp.concatenate([jnp.where(has_before, hp[:halo], 0.0), prod,
                                  jnp.where(has_after, hp[halo:], 0.0)], axis=0)
            sconv = sb_ref[:, _lane_block(blk)]
            for k in range(SC_TAPS):
                off = halo + (k - SC_TAPS // 2) * sc_shift
                sconv = sconv + sw_ref[k:k + 1, _lane_block(blk)] * pe[off:off + tm]
            ysc_ref[:, _lane_block(blk)] = (u_ref[BLOCKS_PER_CHUNK * BG_CHUNK0 + blk, main, :] * sconv).astype(BF16)

    conv = jnp.concatenate([cdst_ref[cb] for cb in range(CONF_WIDTH // LANES)], axis=1)
    mu = jnp.mean(conv, axis=-1, keepdims=True)
    cen = conv - mu
    var = jnp.mean(cen * cen, axis=-1, keepdims=True)
    yconf_ref[...] = jax.nn.silu(cen * lax.rsqrt(var + EPS) * lng_ref[...] + lnb_ref[...]).astype(BF16)


def _inproj_call(h, mod, mod_row, tile, seq, halo, seg, sc_shift, gpre, lw):
    n, d = h.shape
    nt = seq // tile
    tpb = tile // halo
    last_halo_block = n // halo - 1

    def before_map(b, j):
        return (jnp.maximum((b * nt + j) * tpb - 1, 0), 0)

    def after_map(b, j):
        return (jnp.minimum((b * nt + j + 1) * tpb, last_halo_block), 0)

    def row_map(b, j):
        return (b * nt + j, 0)

    consts = [gpre, lw['w_vg'], lw['w_rest'],
              lw['ssd_conv_w'], lw['ssd_conv_b'], lw['dt_bias'],
              lw['conf_w'], lw['conf_b'], lw['ln_g'], lw['ln_b'], lw['sc_w'], lw['sc_b']]
    conf_rows = (tile // seg) * (seg + 2 * CONV_PAD)
    return pl.pallas_call(
        functools.partial(_inproj_kernel, seg=seg, sc_shift=sc_shift),
        grid=(n // seq, nt),
        in_specs=[pl.BlockSpec((tile, d), row_map),
                  pl.BlockSpec((halo, d), before_map),
                  pl.BlockSpec((halo, d), after_map),
                  pl.BlockSpec((1, N_MOD, d), lambda b, j: (mod_row(b), 0, 0))]
                 + [_const_spec(a.shape) for a in consts],
        out_specs=[pl.BlockSpec((tile, XBC_WIDTH), row_map),
                   pl.BlockSpec((2, tile, HEAD_PAD), lambda b, j: (0, b * nt + j, 0)),
                   pl.BlockSpec((tile, SSD_WIDTH), row_map),
                   pl.BlockSpec((tile, CONF_WIDTH), row_map),
                   pl.BlockSpec((tile, SC_WIDTH), row_map)],
        out_shape=[jax.ShapeDtypeStruct((n, XBC_WIDTH), BF16),
                   jax.ShapeDtypeStruct((2, n, HEAD_PAD), F32),
                   jax.ShapeDtypeStruct((n, SSD_WIDTH), BF16),
                   jax.ShapeDtypeStruct((n, CONF_WIDTH), BF16),
                   jax.ShapeDtypeStruct((n, SC_WIDTH), BF16)],
        scratch_shapes=[pltpu.VMEM((tile, d), BF16),
                        pltpu.VMEM((N_CHUNKS * BLOCKS_PER_CHUNK, tile + 2 * CONV_PAD, LANES), F32),
                        pltpu.VMEM((XBC_WIDTH // LANES, tile, LANES), F32),
                        pltpu.VMEM((CONF_WIDTH // LANES, conf_rows, LANES), F32),
                        pltpu.VMEM((CONF_WIDTH // LANES, tile, LANES), F32)],
        compiler_params=_params("parallel", "arbitrary"),
        name="inproj",
    )(h, h, h, mod, *consts)


def _split_dot(v, e):
    hi = v.astype(BF16)
    lo = (v - hi.astype(F32)).astype(BF16)
    return _bdot(hi, e) + _bdot(lo, e)


def _ssd_kernel(xf_ref, xr_ref, dtf_ref, dtr_ref, alog_ref, dskip_ref, h0_ref, yf_ref, yr_ref, st_ref):
    @pl.when(pl.program_id(1) == 0)
    def _():
        st_ref[...] = h0_ref[...]

    q = xf_ref.shape[0]
    x_refs = (xf_ref, xr_ref)
    y_refs = (yf_ref, yr_ref)
    dts = (dtf_ref[0], dtr_ref[0])
    dirs = (0, 1)
    nt_dims = (((1,), (1,)), ((), ()))
    highest = lax.Precision.HIGHEST

    row = lax.broadcasted_iota(jnp.int32, (q, q), 0)
    col = lax.broadcasted_iota(jnp.int32, (q, q), 1)
    tri = (row >= col).astype(F32)
    a = [dts[d] * (-jnp.exp(alog_ref[d])) for d in dirs]
    cs2 = jnp.dot(tri, jnp.concatenate(a, axis=1), precision=highest, preferred_element_type=F32)
    cs = (cs2[:, :HEAD_PAD], cs2[:, HEAD_PAD:])
    tot = [cs[d][q - 1:q] for d in dirs]
    w = (cs[0], tot[1] - cs[1] + a[1])
    causal = (row >= col, row <= col)

    sel = (lax.broadcasted_iota(jnp.int32, (2 * SUBLANES, HEAD_PAD), 0)
           == lax.broadcasted_iota(jnp.int32, (2 * SUBLANES, HEAD_PAD), 1)).astype(F32)
    rows_t = lax.dot_general(sel, jnp.concatenate([w[0], dts[0], w[1], dts[1]], axis=0), nt_dims,
                             precision=highest, preferred_element_type=F32)
    w_t = (rows_t[:, :q], rows_t[:, 2 * q:3 * q])
    dt_t = (rows_t[:, q:2 * q], rows_t[:, 3 * q:])

    expand = (lax.broadcasted_iota(jnp.int32, (HEAD_PAD, SSD_WIDTH), 0)
              == lax.broadcasted_iota(jnp.int32, (HEAD_PAD, SSD_WIDTH), 1) // SSD_HEAD_DIM).astype(BF16)
    scalars = []
    for d in dirs:
        scalars += [jnp.exp(w[d]), jnp.exp(tot[d] - w[d]) * dts[d]]
    scalars += [jnp.broadcast_to(jnp.exp(tot[d]), (SUBLANES, HEAD_PAD)) for d in dirs]
    ex = _split_dot(jnp.concatenate(scalars, axis=0), expand)
    e_in = (ex[:q], ex[2 * q:3 * q])
    e_out = (ex[q:2 * q], ex[3 * q:4 * q])
    e_tot = (ex[4 * q:4 * q + 1], ex[4 * q + SUBLANES:4 * q + SUBLANES + 1])

    left = lax.broadcasted_iota(jnp.int32, (q, LANES), 1) < SSD_HEAD_DIM
    heads_per_group = SSD_HEADS // SSD_GROUPS
    for g in range(SSD_GROUPS):
        gs = slice(g * GROUP_WIDTH, (g + 1) * GROUP_WIDTH)
        b_cols = slice(SSD_WIDTH + g * SSD_STATE, SSD_WIDTH + (g + 1) * SSD_STATE)
        c_cols = slice(SSD_WIDTH + GN + g * SSD_STATE, SSD_WIDTH + GN + (g + 1) * SSD_STATE)
        bg = [x_refs[d][:, b_cols] for d in dirs]
        cg = [x_refs[d][:, c_cols] for d in dirs]
        state = [st_ref[0, d, :, gs] for d in dirs]
        cb = [lax.dot_general(cg[d], bg[d], nt_dims, preferred_element_type=F32) for d in dirs]
        y_off = [_bdot(cg[d], state[d].astype(BF16)) * e_in[d][:, gs] for d in dirs]
        for pair in range(heads_per_group // 2):
            c0 = g * GROUP_WIDTH + pair * LANES
            for d in dirs:
                lhs = []
                for h in (g * heads_per_group + 2 * pair, g * heads_per_group + 2 * pair + 1):
                    decay = jnp.exp(jnp.where(causal[d], w[d][:, h:h + 1] - w_t[d][h:h + 1, :], -1e30))
                    lhs.append((cb[d] * decay * dt_t[d][h:h + 1, :]).astype(BF16))
                xp = x_refs[d][:, c0:c0 + LANES]
                zero = jnp.zeros_like(xp)
                rhs = jnp.concatenate([jnp.where(left, xp, zero), jnp.where(left, zero, xp)], axis=0)
                y = _bdot(jnp.concatenate(lhs, axis=1), rhs) + y_off[d][:, pair * LANES:(pair + 1) * LANES]
                if d == 0:
                    y = y + dskip_ref[:, c0:c0 + LANES] * xp.astype(F32)
                y_refs[d][:, c0:c0 + LANES] = y.astype(BF16)
        for d in dirs:
            xw = (x_refs[d][:, gs].astype(F32) * e_out[d][:, gs]).astype(BF16)
            upd = lax.dot_general(bg[d], xw, (((0,), (0,)), ((), ())), preferred_element_type=F32)
            st_ref[0, d, :, gs] = e_tot[d][:, gs] * state[d] + upd


def _ssd_call(xbc, dt, a_log, d_skip, h0, seq):
    n = xbc.shape[0]
    q = SSD_CHUNK
    nc = seq // q

    def fwd_chunk(b, c):
        return b * nc + c

    def rev_chunk(b, c):
        return b * nc + nc - 1 - c

    return pl.pallas_call(
        _ssd_kernel,
        grid=(n // seq, nc),
        in_specs=[pl.BlockSpec((q, XBC_WIDTH), lambda b, c: (fwd_chunk(b, c), 0)),
                  pl.BlockSpec((q, XBC_WIDTH), lambda b, c: (rev_chunk(b, c), 0)),
                  pl.BlockSpec((1, q, HEAD_PAD), lambda b, c: (0, fwd_chunk(b, c), 0)),
                  pl.BlockSpec((1, q, HEAD_PAD), lambda b, c: (1, rev_chunk(b, c), 0)),
                  pl.BlockSpec((2, 1, HEAD_PAD), lambda b, c: (0, 0, 0)),
                  pl.BlockSpec((1, SSD_WIDTH), lambda b, c: (0, 0)),
                  pl.BlockSpec((1, 2, SSD_STATE, SSD_WIDTH), lambda b, c: (b, 0, 0, 0))],
        out_specs=[pl.BlockSpec((q, SSD_WIDTH), lambda b, c: (fwd_chunk(b, c), 0)),
                   pl.BlockSpec((q, SSD_WIDTH), lambda b, c: (rev_chunk(b, c), 0)),
                   pl.BlockSpec((1, 2, SSD_STATE, SSD_WIDTH), lambda b, c: (b, 0, 0, 0))],
        out_shape=[jax.ShapeDtypeStruct((n, SSD_WIDTH), BF16),
                   jax.ShapeDtypeStruct((n, SSD_WIDTH), BF16),
                   jax.ShapeDtypeStruct(h0.shape, F32)],
        compiler_params=_params("parallel", "arbitrary"),
        name="ssd",
    )(xbc, xbc, dt, dt, a_log, d_skip, h0)


def _post_kernel(h_ref, yf_ref, yr_ref, zs_ref, yconf_ref, ysc_ref, mod_ref, gn_ref, gpost_ref,
                 wo_ssd_ref, wo_conf_ref, wo_sc_ref, o_ref):
    gate = mod_ref[0][5:6]
    y = yf_ref[...].astype(F32) + yr_ref[...].astype(F32)
    y_ssd = _rms(y * zs_ref[...].astype(F32), gn_ref[...])
    o = (_bdot(y_ssd.astype(BF16), wo_ssd_ref[...])
         + _bdot(yconf_ref[...], wo_conf_ref[...])
         + _bdot(ysc_ref[...], wo_sc_ref[...]))
    o_ref[...] = h_ref[...] + gate * _rms(o, gpost_ref[...])


def _post_call(h, yf, yr, zs, yconf, ysc, mod, mod_row, tile, seq, gn, gpost, lw):
    n, d = h.shape
    nt = seq // tile

    def row_map(b, j):
        return (b * nt + j, 0)

    consts = [gn, gpost, lw['wo_ssd'], lw['wo_conf'], lw['wo_sc']]
    return pl.pallas_call(
        _post_kernel,
        grid=(n // seq, nt),
        in_specs=[pl.BlockSpec((tile, d), row_map),
                  pl.BlockSpec((tile, SSD_WIDTH), row_map),
                  pl.BlockSpec((tile, SSD_WIDTH), row_map),
                  pl.BlockSpec((tile, SSD_WIDTH), row_map),
                  pl.BlockSpec((tile, CONF_WIDTH), row_map),
                  pl.BlockSpec((tile, SC_WIDTH), row_map),
                  pl.BlockSpec((1, N_MOD, d), lambda b, j: (mod_row(b), 0, 0))]
                 + [_const_spec(a.shape) for a in consts],
        out_specs=pl.BlockSpec((tile, d), row_map),
        out_shape=jax.ShapeDtypeStruct((n, d), F32),
        compiler_params=_params("parallel", "arbitrary"),
        name="post",
    )(h, yf, yr, zs, yconf, ysc, mod, *consts)


def _pad_lanes(a, width):
    return jnp.pad(a, [(0, 0)] * (a.ndim - 1) + [(0, width - a.shape[-1])])


def _column_chunks(a, width):
    rows, cols = a.shape
    return a.reshape(rows, cols // width, width).transpose(1, 0, 2)


def _layer_weights(i, w_in, ssd_conv_w, ssd_conv_b, ssd_dt_bias, ssd_a_log, ssd_d, conf_conv_w, conf_conv_b,
                   conf_ln_g, conf_ln_b, sc_conv_w, sc_conv_b, w_out):
    wi = w_in[i]
    dt_cols = wi[:, XBC_WIDTH:SSD_IN]
    w_dt = jnp.concatenate([_pad_lanes(dt_cols[:, :SSD_HEADS], HEAD_PAD),
                            _pad_lanes(dt_cols[:, SSD_HEADS:], HEAD_PAD)], axis=1)
    dtb = ssd_dt_bias[i]
    wo = w_out[i]
    conf0 = SSD_IN + SSD_WIDTH
    sc0 = conf0 + 2 * CONF_WIDTH
    w_rest = jnp.concatenate([wi[:, :XBC_WIDTH], w_dt, wi[:, SSD_IN:conf0], wi[:, sc0:]], axis=1)
    return {
        'w_vg': wi[:, conf0:conf0 + 2 * CONF_WIDTH],
        'w_rest': _column_chunks(w_rest, MXU_DIM),
        'ssd_conv_w': _column_chunks(ssd_conv_w[i], LANES),
        'ssd_conv_b': _column_chunks(ssd_conv_b[i][None], LANES),
        'dt_bias': jnp.concatenate([_pad_lanes(dtb[0], HEAD_PAD), _pad_lanes(dtb[1], HEAD_PAD)])[None],
        'a_log': _pad_lanes(ssd_a_log[i], HEAD_PAD)[:, None, :],
        'd_skip': jnp.repeat(ssd_d[i], SSD_HEAD_DIM)[None],
        'conf_w': _column_chunks(conf_conv_w[i], LANES),
        'conf_b': _column_chunks(conf_conv_b[i][None], LANES),
        'ln_g': conf_ln_g[i][None], 'ln_b': conf_ln_b[i][None],
        'sc_w': sc_conv_w[i], 'sc_b': sc_conv_b[i][None],
        'wo_ssd': wo[:SSD_WIDTH],
        'wo_conf': wo[SSD_WIDTH:SSD_WIDTH + CONF_WIDTH],
        'wo_sc': wo[SSD_WIDTH + CONF_WIDTH:],
    }


def kernel(x, c, ctx, c_ctx, w_mod, b_mod, norm_pre, norm_post, ffn_w_gate, ffn_w_up, ffn_w_down,
           w_in, ssd_conv_w, ssd_conv_b, ssd_dt_bias, ssd_a_log, ssd_d, ssd_norm,
           conf_conv_w, conf_conv_b, conf_ln_g, conf_ln_b, sc_conv_w, sc_conv_b, w_out):
    bsz, seq, d = x.shape
    ctx_len = ctx.shape[1]
    depth = w_mod.shape[0]

    ffn_w_gate, ffn_w_up, w_in, w_out = [_to_bf16(a, 2) for a in (ffn_w_gate, ffn_w_up, w_in, w_out)]
    ffn_w_down = _to_bf16(ffn_w_down, 4)

    rows = jnp.concatenate([c, c_ctx[None]], axis=0)
    n_rows = -(-rows.shape[0] // SUBLANES) * SUBLANES
    rows = jnp.pad(rows, ((0, n_rows - rows.shape[0]), (0, 0)))
    mod_all = _mod_call(rows, w_mod, b_mod).reshape(depth, n_rows, N_MOD, d)

    def x_row(b):
        return b

    def ctx_row(b):
        return bsz

    h = x.reshape(bsz * seq, d)
    hc = ctx.reshape(bsz * ctx_len, d)
    zero_state = jnp.zeros((bsz, 2, SSD_STATE, SSD_WIDTH), F32)

    for i in range(depth):
        last = i == depth - 1
        mod = mod_all[i]
        lw = _layer_weights(i, w_in, ssd_conv_w, ssd_conv_b, ssd_dt_bias, ssd_a_log, ssd_d, conf_conv_w,
                            conf_conv_b, conf_ln_g, conf_ln_b, sc_conv_w, sc_conv_b, w_out)
        ffn_w = [(ffn_w_gate[i, k], ffn_w_up[i, k], ffn_w_down[i, k]) for k in range(2)]
        gpre = [norm_pre[i, k][None] for k in range(3)]
        gpost = [norm_post[i, k][None] for k in range(3)]
        gn = ssd_norm[i][None]

        h = _ffn_call(h, mod, x_row, TOKEN_TILE, seq, 0, gpre[0], gpost[0], *ffn_w[0])
        hc = _ffn_call(hc, mod, ctx_row, ctx_len, ctx_len, 0, gpre[0], gpost[0], *ffn_w[0])

        xbc, dt, zs, yconf, ysc = _inproj_call(hc, mod, ctx_row, ctx_len, ctx_len, CONV_PAD, ctx_len, 1,
                                               gpre[1], lw)
        yf, yr, state = _ssd_call(xbc, dt, lw['a_log'], lw['d_skip'], zero_state, ctx_len)
        if not last:
            hc = _post_call(hc, yf, yr, zs, yconf, ysc, mod, ctx_row, ctx_len, ctx_len, gn, gpost[1], lw)
            hc = _ffn_call(hc, mod, ctx_row, ctx_len, ctx_len, 2, gpre[2], gpost[2], *ffn_w[1])

        xbc, dt, zs, yconf, ysc = _inproj_call(h, mod, x_row, TOKEN_TILE, seq, GRID_W, GRID_W, GRID_W,
                                               gpre[1], lw)
        yf, yr, _ = _ssd_call(xbc, dt, lw['a_log'], lw['d_skip'], state, seq)
        h = _post_call(h, yf, yr, zs, yconf, ysc, mod, x_row, TOKEN_TILE, seq, gn, gpost[1], lw)
        h = _ffn_call(h, mod, x_row, TOKEN_TILE, seq, 2, gpre[2], gpost[2], *ffn_w[1])

    return h.reshape(bsz, seq, d)
```
